```python
import math
import jax, jax.numpy as jnp
from jax import lax
import numpy as np


D_MODEL = 1024
BATCH = 8
SEQ = 8192
DEPTH = 4

CTX_LEN = 256
GRID_W = 64

W_A = D_MODEL // 4
A_GROUPS = 4
A_CHUNK = 128
W_B = D_MODEL // 4
CONV_W = 3
HEAD_DIM = 64
W_C = D_MODEL // 2
N_HEADS = W_C // HEAD_DIM
N_KV = 2
WINDOW = 128
BLOCK = 128
MIX_WIDTH = W_A + W_B + W_C
ROPE_BASE = 10000.0
QUARTER = HEAD_DIM // 4
ATTN_SCALE = HEAD_DIM ** -0.5
EPS = 1e-6

SPLITS = (W_A, 2 * W_A, 2 * W_A + W_B, 2 * W_A + 2 * W_B, 2 * W_A + 3 * W_B,
          2 * W_A + 3 * W_B + W_C, 2 * W_A + 3 * W_B + W_C + N_KV * HEAD_DIM)
KV_START = 2 * W_A + 3 * W_B + W_C
IN_COLS = KV_START + 2 * N_KV * HEAD_DIM

N_EXPERTS = 16
N_GROUPS = 4
EXPERTS_PER_GROUP = N_EXPERTS // N_GROUPS
TOP_K = 2
D_EXPERT = 512

kernel_name = "hybrid_parallel_mixer_diffusion_trunk"


def rms_norm(x, g):
    xf = x.astype(jnp.float32)
    y = xf * lax.rsqrt(jnp.mean(xf * xf, axis=-1, keepdims=True) + EPS)
    return (y * g.astype(jnp.float32)).astype(x.dtype)


def modulation(cvec, w_mod, b_mod):
    m = (jax.nn.silu(cvec) @ w_mod + b_mod)[:, None, :]
    return jnp.split(m, 6, axis=-1)


def axial_rope_tables(n, dtype):
    rows = n // GRID_W
    row = jnp.repeat(jnp.arange(rows, dtype=jnp.float32), GRID_W)
    col = jnp.tile(jnp.arange(GRID_W, dtype=jnp.float32), rows)
    inv = ROPE_BASE ** (-jnp.arange(QUARTER, dtype=jnp.float32) / QUARTER)
    ang = jnp.stack([row[:, None] * inv, col[:, None] * inv], axis=1)
    ang = ang[:, None]
    return jnp.cos(ang).astype(dtype), jnp.sin(ang).astype(dtype)


def apply_axial_rope(t, cos, sin):
    tr = t.reshape(t.shape[:-1] + (2, 2, QUARTER))
    t1, t2 = tr[..., 0, :], tr[..., 1, :]
    o1 = t1 * cos - t2 * sin
    o2 = t2 * cos + t1 * sin
    return jnp.stack([o1, o2], axis=-2).reshape(t.shape)


def chunk_gmlp(u, v, g_v, w_s, b_s):
    b_, n, _ = u.shape
    u = jax.nn.gelu(u)
    v = rms_norm(jax.nn.gelu(v), g_v)
    v = v.reshape(b_, n // A_CHUNK, A_CHUNK, A_GROUPS, W_A // A_GROUPS)
    s = jnp.einsum('hpq,bcqhd->bcphd', w_s, v) + b_s.T[None, None, :, :, None]
    return u * s.reshape(b_, n, W_A)


def short_gated_conv(bg, cg, h, w_conv):
    z = cg * h
    n = z.shape[1]
    zp = jnp.pad(z, ((0, 0), (1, 1), (0, 0)))
    y = zp[:, :n] * w_conv[0] + zp[:, 1:n + 1] * w_conv[1] + zp[:, 2:] * w_conv[2]
    return bg * y


def context_attention(q, k, v, sink):
    b_, L = q.shape[:2]
    g = N_HEADS // N_KV
    qg = q.reshape(b_, L, N_KV, g, HEAD_DIM)
    s = jnp.einsum('bqkgd,bskd->bkgqs', qg, k, preferred_element_type=jnp.float32) * ATTN_SCALE
    s_sink = jnp.broadcast_to(sink.astype(jnp.float32).reshape(1, N_KV, g, 1, 1), s.shape[:-1] + (1,))
    p = jax.nn.softmax(jnp.concatenate([s, s_sink], axis=-1), axis=-1)[..., :L].astype(v.dtype)
    o = jnp.einsum('bkgqs,bskd->bqkgd', p, v)
    return o.reshape(b_, L, N_HEADS * HEAD_DIM)


def window_attention(q, k, v, k_ctx, v_ctx, sink):
    b_, n = q.shape[:2]
    nb = n // BLOCK
    g = N_HEADS // N_KV
    L = k_ctx.shape[1]
    nl = 3 * BLOCK
    qb = q.reshape(b_, nb, BLOCK, N_KV, g, HEAD_DIM)

    def band(t):
        tp = jnp.pad(t.reshape(b_, nb, BLOCK, N_KV, HEAD_DIM), ((0, 0), (1, 1), (0, 0), (0, 0), (0, 0)))
        return jnp.concatenate([tp[:, :-2], tp[:, 1:-1], tp[:, 2:]], axis=2)

    kb, vb = band(k), band(v)
    s_loc = jnp.einsum('bnqkgd,bnskd->bnkgqs', qb, kb, preferred_element_type=jnp.float32) * ATTN_SCALE
    blk = jnp.arange(nb)[:, None, None]
    qpos = blk * BLOCK + jnp.arange(BLOCK)[None, :, None]
    kpos = (blk - 1) * BLOCK + jnp.arange(nl)[None, None, :]
    valid = (jnp.abs(qpos - kpos) <= WINDOW) & (kpos >= 0) & (kpos < n)
    s_loc = jnp.where(valid[None, :, None, None], s_loc, -jnp.inf)
    s_ctx = jnp.einsum('bnqkgd,bskd->bnkgqs', qb, k_ctx, preferred_element_type=jnp.float32) * ATTN_SCALE
    s_sink = jnp.broadcast_to(sink.astype(jnp.float32).reshape(1, 1, N_KV, g, 1, 1), s_ctx.shape[:-1] + (1,))
    p = jax.nn.softmax(jnp.concatenate([s_loc, s_ctx, s_sink], axis=-1), axis=-1).astype(v.dtype)
    o = (jnp.einsum('bnkgqs,bnskd->bnqkgd', p[..., :nl], vb)
         + jnp.einsum('bnkgqs,bskd->bnqkgd', p[..., nl:nl + L], v_ctx))
    return o.reshape(b_, n, N_HEADS * HEAD_DIM)


def grouped_moe(h, router_w, router_b, w_gate, w_up, w_down):
    n = h.shape[0]
    scores = jax.nn.sigmoid(jnp.einsum('nd,de->ne', h, router_w, preferred_element_type=jnp.float32))
    sel = (scores + router_b.astype(jnp.float32)).reshape(n, N_GROUPS, EXPERTS_PER_GROUP)
    group_score = jnp.sum(lax.top_k(sel, TOP_K)[0], axis=-1)
    best_group = jnp.argmax(group_score, axis=-1)
    in_group = jnp.take_along_axis(sel, best_group[:, None, None], axis=1)[:, 0]
    _, top_idx = lax.top_k(in_group, TOP_K)
    expert_idx = best_group[:, None] * EXPERTS_PER_GROUP + top_idx
    w = jnp.take_along_axis(scores, expert_idx, axis=-1)
    w = w / jnp.sum(w, axis=-1, keepdims=True)
    gates = jnp.sum(jax.nn.one_hot(expert_idx, N_EXPERTS, dtype=jnp.float32) * w[..., None], axis=1).astype(h.dtype)
    out = jnp.zeros_like(h)
    for e in range(N_EXPERTS):
        a = jax.nn.silu(h @ w_gate[e]) * (h @ w_up[e])
        out = out + gates[:, e:e + 1] * (a @ w_down[e])
    return out


def trunk_layer(x, xc, mod_x, mod_c, norm1_g, norm2_g, w_in, gmlp_g, w_s, b_s, conv_w, sink,
                w_out, router_w, router_b, w_gate, w_up, w_down, cos, sin, ctx_out):
    b_, n, d = x.shape
    L = xc.shape[1]
    sh1, sc1, g1, sh2, sc2, g2 = mod_x
    csh1, csc1, cg1, csh2, csc2, cg2 = mod_c

    hx = rms_norm(x, norm1_g) * (1 + sc1) + sh1
    hc = rms_norm(xc, norm1_g) * (1 + csc1) + csh1
    ux, vx, bx, cx, zx, qx, kx, vvx = jnp.split(hx @ w_in, SPLITS, axis=-1)
    qx = apply_axial_rope(qx.reshape(b_, n, N_HEADS, HEAD_DIM), cos, sin)
    kx = apply_axial_rope(kx.reshape(b_, n, N_KV, HEAD_DIM), cos, sin)
    vvx = vvx.reshape(b_, n, N_KV, HEAD_DIM)
    if ctx_out:
        uc, vc, bc, cc, zc, qc, kc, vvc = jnp.split(hc @ w_in, SPLITS, axis=-1)
    else:
        kc, vvc = jnp.split(hc @ w_in[:, KV_START:], 2, axis=-1)
    kc = kc.reshape(b_, L, N_KV, HEAD_DIM)
    vvc = vvc.reshape(b_, L, N_KV, HEAD_DIM)

    mix_x = jnp.concatenate([
        chunk_gmlp(ux, vx, gmlp_g, w_s, b_s),
        short_gated_conv(bx, cx, zx, conv_w),
        window_attention(qx, kx, vvx, kc, vvc, sink)], axis=-1) @ w_out
    x = x + g1 * mix_x
    if ctx_out:
        mix_c = jnp.concatenate([
            chunk_gmlp(uc, vc, gmlp_g, w_s, b_s),
            short_gated_conv(bc, cc, zc, conv_w),
            context_attention(qc.reshape(b_, L, N_HEADS, HEAD_DIM), kc, vvc, sink)], axis=-1) @ w_out
        xc = xc + cg1 * mix_c

    h2x = rms_norm(x, norm2_g) * (1 + sc2) + sh2
    if ctx_out:
        h2c = rms_norm(xc, norm2_g) * (1 + csc2) + csh2
        tokens = jnp.concatenate([h2x.reshape(-1, d), h2c.reshape(-1, d)], axis=0)
        y = grouped_moe(tokens, router_w, router_b, w_gate, w_up, w_down)
        x = x + g2 * y[:b_ * n].reshape(b_, n, d)
        xc = xc + cg2 * y[b_ * n:].reshape(b_, L, d)
    else:
        y = grouped_moe(h2x.reshape(-1, d), router_w, router_b, w_gate, w_up, w_down)
        x = x + g2 * y.reshape(b_, n, d)
    return x, xc


def setup_inputs(seed: int = 0) -> dict:
    key = jax.random.key(seed)
    ks = jax.random.split(key, 24)
    f32 = jnp.float32

    def nrm(k, shape, s):
        return jax.random.normal(k, shape, f32) * s

    return {
        "x": nrm(ks[0], (BATCH, SEQ, D_MODEL), 1.0),
        "c": nrm(ks[1], (BATCH, D_MODEL), 1.0),
        "ctx": nrm(ks[2], (BATCH, CTX_LEN, D_MODEL), 1.0),
        "c_ctx": nrm(ks[3], (D_MODEL,), 1.0),
        "w_mod": nrm(ks[4], (DEPTH, D_MODEL, 6 * D_MODEL), 0.5 * D_MODEL ** -0.5),
        "b_mod": nrm(ks[5], (DEPTH, 6 * D_MODEL), 0.02),
        "norm1_g": 1.0 + nrm(ks[6], (DEPTH, D_MODEL), 0.02),
        "norm2_g": 1.0 + nrm(ks[7], (DEPTH, D_MODEL), 0.02),
        "w_in": nrm(ks[8], (DEPTH, D_MODEL, IN_COLS), D_MODEL ** -0.5),
        "gmlp_g": 1.0 + nrm(ks[9], (DEPTH, W_A), 0.02),
        "w_s": nrm(ks[10], (DEPTH, A_GROUPS, A_CHUNK, A_CHUNK), A_CHUNK ** -0.5),
        "b_s": nrm(ks[11], (DEPTH, A_GROUPS, A_CHUNK), 0.02),
        "conv_w": nrm(ks[12], (DEPTH, CONV_W, W_B), CONV_W ** -0.5),
        "attn_sink": nrm(ks[13], (DEPTH, N_HEADS), 0.5),
        "w_out": nrm(ks[14], (DEPTH, MIX_WIDTH, D_MODEL), MIX_WIDTH ** -0.5),
        "router_w": nrm(ks[15], (D_MODEL, N_EXPERTS), D_MODEL ** -0.5),
        "router_b": nrm(ks[16], (N_EXPERTS,), 0.01),
        "w_gate": nrm(ks[17], (DEPTH, N_EXPERTS, D_MODEL, D_EXPERT), D_MODEL ** -0.5),
        "w_up": nrm(ks[18], (DEPTH, N_EXPERTS, D_MODEL, D_EXPERT), D_MODEL ** -0.5),
        "w_down": nrm(ks[19], (DEPTH, N_EXPERTS, D_EXPERT, D_MODEL), D_EXPERT ** -0.5),
        "final_g": 1.0 + nrm(ks[20], (D_MODEL,), 0.02),
    }


def reference(x, c, ctx, c_ctx, w_mod, b_mod, norm1_g, norm2_g, w_in, gmlp_g, w_s, b_s, conv_w,
              attn_sink, w_out, router_w, router_b, w_gate, w_up, w_down, final_g):
    n = x.shape[1]
    cos, sin = axial_rope_tables(n, x.dtype)
    xc = ctx
    for layer in range(DEPTH):
        mod_x = modulation(c, w_mod[layer], b_mod[layer])
        mod_c = modulation(c_ctx[None], w_mod[layer], b_mod[layer])
        x, xc = trunk_layer(x, xc, mod_x, mod_c, norm1_g[layer], norm2_g[layer], w_in[layer],
                            gmlp_g[layer], w_s[layer], b_s[layer], conv_w[layer], attn_sink[layer],
                            w_out[layer], router_w, router_b, w_gate[layer], w_up[layer], w_down[layer],
                            cos, sin, layer < DEPTH - 1)
    return rms_norm(x, final_g)
```

```python
import functools

import jax
import jax.numpy as jnp
from jax import lax
from jax.experimental import pallas as pl
from jax.experimental.pallas import tpu as pltpu

F32 = jnp.float32
BF16 = jnp.bfloat16

D_MODEL = 1024
W_A = 256
A_GROUPS = 4
A_CHUNK = 128
W_B = 256
HEAD_DIM = 64
W_C = 512
N_HEADS = 8
N_KV = 2
BLOCK = 128
GRID_W = 64
QUARTER = 16
ROPE_BASE = 10000.0
ATTN_SCALE = HEAD_DIM ** -0.5
EPS = 1e-6
N_EXPERTS = 16
N_GROUPS = 4
EXPERTS_PER_GROUP = 4
D_EXPERT = 512
N_PAIRS = 6
N_CLASSES = N_GROUPS * N_PAIRS
PAIRS = ((0, 1), (0, 2), (0, 3), (1, 2), (1, 3), (2, 3))

LANES = 128
SUBLANES = 8
MOD_ROWS = 16
PROJ_COLS = 2048
P_Z, P_K, P_V, P_U, P_GV, P_BG, P_Q = 0, 256, 384, 512, 768, 1024, 1280
P_COLS = 1792
HALO_COLS = 512

TT_PROJ = 512
TT_MIX = 512
TM_MOE = 256
VMEM_LIMIT = 52 * 1024 * 1024


def _cparams(n_axes):
    return pltpu.CompilerParams(
        dimension_semantics=("arbitrary",) * n_axes, vmem_limit_bytes=VMEM_LIMIT)


def _mod_body(c_ref, w_ref, b_ref, o_ref):
    a = jax.nn.silu(c_ref[...]).astype(BF16)
    o_ref[...] = jnp.dot(a, w_ref[...].astype(BF16), preferred_element_type=F32) + b_ref[...]


def _modulation(c_all, w_mod, b_mod):
    depth = w_mod.shape[0]
    tn = 1536
    return pl.pallas_call(
        _mod_body,
        grid=(depth, 6 * D_MODEL // tn),
        in_specs=[
            pl.BlockSpec((MOD_ROWS, D_MODEL), lambda l, n: (0, 0)),
            pl.BlockSpec((None, D_MODEL, tn), lambda l, n: (l, 0, n)),
            pl.BlockSpec((None, 1, tn), lambda l, n: (l, 0, n)),
        ],
        out_specs=pl.BlockSpec((None, MOD_ROWS, tn), lambda l, n: (l, 0, n)),
        out_shape=jax.ShapeDtypeStruct((depth, MOD_ROWS, 6 * D_MODEL), F32),
        compiler_params=_cparams(2),
        name="modulation",
    )(c_all, w_mod, b_mod.reshape(depth, 1, 6 * D_MODEL))


def _rms(x, g):
    return x * lax.rsqrt(jnp.mean(x * x, axis=-1, keepdims=True) + EPS) * g


def _rope(t, c, s1, s2):
    return t * c + pltpu.roll(t, LANES - QUARTER, axis=1) * s1 + pltpu.roll(t, QUARTER, axis=1) * s2


def _inproj_body(has_y, *refs):
    if has_y:
        (x_ref, y_ref, mod_ref, modp_ref, n1g_ref, gg_ref, win_ref, c_ref, s1_ref, s2_ref,
         xo_ref, p_ref) = refs
        x = x_ref[...] + modp_ref[5:6, :] * y_ref[...]
        xo_ref[...] = x
    else:
        x_ref, mod_ref, n1g_ref, gg_ref, win_ref, c_ref, s1_ref, s2_ref, p_ref = refs
        x = x_ref[...]
    hx = _rms(x, n1g_ref[...]) * (1.0 + mod_ref[1:2, :]) + mod_ref[0:1, :]
    hb = hx.astype(BF16)

    def proj(lo, hi):
        return jnp.dot(hb, win_ref[:, lo:hi], preferred_element_type=F32)

    c, s1, s2 = c_ref[...], s1_ref[...], s2_ref[...]
    p_ref[:, P_Z:P_Z + W_B] = (proj(0, 256) * proj(256, 512)).astype(BF16)
    p_ref[:, P_K:P_K + LANES] = _rope(proj(512, 640), c, s1, s2).astype(BF16)
    p_ref[:, P_V:P_V + LANES] = proj(640, 768).astype(BF16)
    p_ref[:, P_U:P_U + W_A] = jax.nn.gelu(proj(768, 1024)).astype(BF16)
    p_ref[:, P_GV:P_GV + W_A] = _rms(jax.nn.gelu(proj(1024, 1280)), gg_ref[...]).astype(BF16)
    p_ref[:, P_BG:P_BG + W_B] = proj(1280, 1536).astype(BF16)
    for i in range(W_C // LANES):
        q = _rope(proj(1536 + i * LANES, 1536 + (i + 1) * LANES), c, s1, s2) * ATTN_SCALE
        p_ref[:, P_Q + i * LANES:P_Q + (i + 1) * LANES] = q.astype(BF16)


def _inproj(layer, x_all, y_all, mod4, n1g, gg, win, rope_tabs, n_lat, seq, batch):
    n_all = x_all.shape[0]
    tt = TT_PROJ
    tiles_per_batch = seq // tt
    n_lat_tiles = n_lat // tt
    has_y = y_all is not None

    def mod_row(i):
        return jnp.minimum(i // tiles_per_batch, batch)

    def rope_blk(i):
        return jnp.where(i < n_lat_tiles, i % tiles_per_batch, tiles_per_batch)

    row_spec = pl.BlockSpec((tt, D_MODEL), lambda i: (i, 0))
    in_specs = [row_spec]
    args = [x_all]
    if has_y:
        in_specs.append(row_spec)
        args.append(y_all)
    in_specs.append(pl.BlockSpec((None, None, 6, D_MODEL), lambda i: (layer, mod_row(i), 0, 0)))
    args.append(mod4)
    if has_y:
        in_specs.append(
            pl.BlockSpec((None, None, 6, D_MODEL), lambda i: (layer - 1, mod_row(i), 0, 0)))
        args.append(mod4)
    in_specs += [
        pl.BlockSpec((1, D_MODEL), lambda i: (0, 0)),
        pl.BlockSpec((1, W_A), lambda i: (0, 0)),
        pl.BlockSpec((D_MODEL, PROJ_COLS), lambda i: (0, 0)),
    ]
    args += [n1g, gg, win]
    for tab in rope_tabs:
        in_specs.append(pl.BlockSpec((tt, LANES), lambda i: (rope_blk(i), 0)))
        args.append(tab)
    p_shape = jax.ShapeDtypeStruct((n_all, P_COLS), BF16)
    p_spec = pl.BlockSpec((tt, P_COLS), lambda i: (i, 0))
    if has_y:
        out_shape = (jax.ShapeDtypeStruct((n_all, D_MODEL), F32), p_shape)
        out_specs = (row_spec, p_spec)
    else:
        out_shape, out_specs = p_shape, p_spec
    out = pl.pallas_call(
        functools.partial(_inproj_body, has_y),
        grid=(n_all // tt,),
        in_specs=in_specs, out_specs=out_specs, out_shape=out_shape,
        compiler_params=_cparams(1),
        name="inproj",
    )(*args)
    return out if has_y else (x_all, out)


def _route(logits_t, rb):
    sc = jax.nn.sigmoid(logits_t)
    sel = sc + rb
    sel_r = [sel[e:e + 1, :] for e in range(N_EXPERTS)]
    sc_r = [sc[e:e + 1, :] for e in range(N_EXPERTS)]
    best = None
    for g in range(N_GROUPS):
        v = sel_r[g * 4:g * 4 + 4]
        gs = None
        for (i, j) in PAIRS:
            pair = v[i] + v[j]
            gs = pair if gs is None else jnp.maximum(gs, pair)
        if best is None:
            best_score, best = gs, jnp.zeros_like(gs)
            vb = list(v)
            sb = list(sc_r[0:4])
        else:
            better = gs > best_score
            best_score = jnp.where(better, gs, best_score)
            best = jnp.where(better, float(g), best)
            vb = [jnp.where(better, v[i], vb[i]) for i in range(4)]
            sb = [jnp.where(better, sc_r[g * 4 + i], sb[i]) for i in range(4)]
    chosen = []
    for i in range(4):
        rank = jnp.zeros_like(best)
        for j in range(4):
            if j == i:
                continue
            beats = (vb[j] >= vb[i]) if j < i else (vb[j] > vb[i])
            rank = rank + jnp.where(beats, 1.0, 0.0)
        chosen.append(rank < 2.0)
    pair_id = jnp.zeros_like(best)
    w_lo = jnp.zeros_like(best)
    w_hi = jnp.zeros_like(best)
    for p, (i, j) in enumerate(PAIRS):
        hit = chosen[i] & chosen[j]
        tot = sb[i] + sb[j]
        pair_id = jnp.where(hit, float(p), pair_id)
        w_lo = jnp.where(hit, sb[i] / tot, w_lo)
        w_hi = jnp.where(hit, sb[j] / tot, w_hi)
    return best * float(N_PAIRS) + pair_id, w_lo, w_hi


def _gmlp_rows(p_ref, ws_ref, bs_ref, mix_ref, row0, n_rows):
    lane_grp = lax.broadcasted_iota(jnp.int32, (A_CHUNK, W_A), 1) // (W_A // A_GROUPS)
    for c in range(n_rows // A_CHUNK):
        rows = slice(row0 + c * A_CHUNK, row0 + (c + 1) * A_CHUNK)
        vch = p_ref[rows, P_GV:P_GV + W_A]
        s = bs_ref[...]
        for h in range(A_GROUPS):
            sh = jnp.dot(ws_ref[h], vch, preferred_element_type=F32)
            s = s + jnp.where(lane_grp == h, sh, 0.0)
        mix_ref[rows, 0:W_A] = (p_ref[rows, P_U:P_U + W_A].astype(F32) * s).astype(BF16)


def _conv_rows(p_ref, cw_ref, mix_ref, row0, n_rows, prev_row, next_row):
    rows = slice(row0, row0 + n_rows)
    z = p_ref[rows, P_Z:P_Z + W_B].astype(F32)
    ridx = lax.broadcasted_iota(jnp.int32, (n_rows, W_B), 0)
    z_up = jnp.where(ridx == 0, prev_row, pltpu.roll(z, 1, axis=0))
    z_dn = jnp.where(ridx == n_rows - 1, next_row, pltpu.roll(z, n_rows - 1, axis=0))
    conv = z_up * cw_ref[0:1, :] + z * cw_ref[1:2, :] + z_dn * cw_ref[2:3, :]
    mix_ref[rows, W_A:W_A + W_B] = (p_ref[rows, P_BG:P_BG + W_B].astype(F32) * conv).astype(BF16)


def _dup_heads(t):
    lane_lo = lax.broadcasted_iota(jnp.int32, (1, LANES), 1) < HEAD_DIM
    tf = t.astype(F32)
    tsw = pltpu.roll(tf, HEAD_DIM, axis=1)
    return (jnp.where(lane_lo, tf, tsw).astype(BF16), jnp.where(lane_lo, tsw, tf).astype(BF16))


def _attend_rows(p_ref, sink_ref, mix_ref, row0, n_qblocks, kc_d, vc_d, local):
    lane_lo = lax.broadcasted_iota(jnp.int32, (1, LANES), 1) < HEAD_DIM
    row_even = lax.broadcasted_iota(jnp.int32, (2 * BLOCK, 1), 0) < BLOCK
    zero_q = jnp.zeros((BLOCK, LANES), BF16)
    nt_dims = (((1,), (1,)), ((), ()))
    n_loc = 3 * BLOCK
    if local is not None:
        kl_d, vl_d, first_tile, last_tile = local
        col = lax.broadcasted_iota(jnp.int32, (2 * BLOCK, n_loc), 1)
        qrow = lax.broadcasted_iota(jnp.int32, (2 * BLOCK, n_loc), 0) % BLOCK
    for qb in range(n_qblocks):
        rows = slice(row0 + qb * BLOCK, row0 + (qb + 1) * BLOCK)
        if local is not None:
            lo_b = jnp.where(first_tile & (qb == 0), BLOCK, 0)
            hi_b = jnp.where(last_tile & (qb == n_qblocks - 1), 2 * BLOCK, 3 * BLOCK)
            valid = (col >= jnp.maximum(qrow, lo_b)) & (col <= jnp.minimum(qrow + 2 * BLOCK, hi_b - 1))
        for i in range(W_C // LANES):
            kv = i // 2
            q = p_ref[rows, P_Q + i * LANES:P_Q + (i + 1) * LANES]
            qz = jnp.concatenate([jnp.where(lane_lo, q, zero_q), jnp.where(lane_lo, zero_q, q)], 0)
            sk = jnp.where(row_even, sink_ref[2 * i], sink_ref[2 * i + 1])
            s_ctx = lax.dot_general(qz, kc_d[kv], nt_dims, preferred_element_type=F32)
            m = jnp.maximum(jnp.max(s_ctx, axis=1, keepdims=True), sk)
            if local is not None:
                k_loc = kl_d[kv][qb * BLOCK:qb * BLOCK + n_loc]
                s_loc = lax.dot_general(qz, k_loc, nt_dims, preferred_element_type=F32)
                s_loc = jnp.where(valid, s_loc, -jnp.inf)
                m = jnp.maximum(m, jnp.max(s_loc, axis=1, keepdims=True))
            p_ctx = jnp.exp(s_ctx - m)
            den = jnp.sum(p_ctx, axis=1, keepdims=True) + jnp.exp(sk - m)
            o = jnp.dot(p_ctx.astype(BF16), vc_d[kv], preferred_element_type=F32)
            if local is not None:
                p_loc = jnp.exp(s_loc - m)
                den = den + jnp.sum(p_loc, axis=1, keepdims=True)
                v_loc = vl_d[kv][qb * BLOCK:qb * BLOCK + n_loc]
                o = o + jnp.dot(p_loc.astype(BF16), v_loc, preferred_element_type=F32)
            o = o / den
            slab = jnp.where(lane_lo, o[0:BLOCK], o[BLOCK:2 * BLOCK])
            c0 = W_A + W_B + i * LANES
            mix_ref[rows, c0:c0 + LANES] = slab.astype(BF16)


def _mixer_body(tt, n_seq_tiles, n_lat_tiles, ctx_len, has_ctx,
                p_ref, hp_ref, hn_ref, pc_ref, x_ref, mod_ref, ws_ref, bs_ref, cw_ref, sink_ref,
                wout_ref, n2g_ref, rwt_ref, rb_ref, xo_ref, h2_ref, ro_ref, mix_ref):
    t = pl.program_id(0)

    def latent_tile():
        j = t % n_seq_tiles
        first_tile, last_tile = j == 0, j == n_seq_tiles - 1
        _gmlp_rows(p_ref, ws_ref, bs_ref, mix_ref, 0, tt)
        prev_row = hp_ref[BLOCK - 16:BLOCK, P_Z:P_Z + W_B].astype(F32)[15:16, :]
        next_row = hn_ref[0:16, P_Z:P_Z + W_B].astype(F32)[0:1, :]
        _conv_rows(p_ref, cw_ref, mix_ref, 0, tt,
                   jnp.where(first_tile, 0.0, prev_row), jnp.where(last_tile, 0.0, next_row))
        k_ext = jnp.concatenate(
            [hp_ref[:, P_K:P_K + LANES], p_ref[:, P_K:P_K + LANES], hn_ref[:, P_K:P_K + LANES]], 0)
        v_ext = jnp.concatenate(
            [hp_ref[:, P_V:P_V + LANES], p_ref[:, P_V:P_V + LANES], hn_ref[:, P_V:P_V + LANES]], 0)
        _attend_rows(p_ref, sink_ref, mix_ref, 0, tt // BLOCK,
                     _dup_heads(pc_ref[:, P_K:P_K + LANES]), _dup_heads(pc_ref[:, P_V:P_V + LANES]),
                     (_dup_heads(k_ext), _dup_heads(v_ext), first_tile, last_tile))

    def context_tile():
        zero_row = jnp.zeros((1, W_B), F32)
        for seg in range(tt // ctx_len):
            r0 = seg * ctx_len
            _gmlp_rows(p_ref, ws_ref, bs_ref, mix_ref, r0, ctx_len)
            _conv_rows(p_ref, cw_ref, mix_ref, r0, ctx_len, zero_row, zero_row)
            _attend_rows(p_ref, sink_ref, mix_ref, r0, ctx_len // BLOCK,
                         _dup_heads(p_ref[r0:r0 + ctx_len, P_K:P_K + LANES]),
                         _dup_heads(p_ref[r0:r0 + ctx_len, P_V:P_V + LANES]), None)

    if has_ctx:
        pl.when(t < n_lat_tiles)(latent_tile)
        pl.when(t >= n_lat_tiles)(context_tile)
    else:
        latent_tile()

    mixed = jnp.dot(mix_ref[...], wout_ref[...], preferred_element_type=F32)
    x = x_ref[...] + mod_ref[2:3, :] * mixed
    xo_ref[...] = x
    h2 = _rms(x, n2g_ref[...]) * (1.0 + mod_ref[4:5, :]) + mod_ref[3:4, :]
    h2_ref[...] = h2
    logits_t = lax.dot_general(rwt_ref[...], h2.astype(BF16), (((1,), (1,)), ((), ())),
                               preferred_element_type=F32)
    cls, w_lo, w_hi = _route(logits_t, rb_ref[...])
    ro_ref[...] = jnp.zeros((8, tt), F32)
    ro_ref[0:1, :] = cls
    ro_ref[1:2, :] = w_lo
    ro_ref[2:3, :] = w_hi


def _mixer(layer, has_ctx, p_all, x_all, mod4, wts, n_lat, seq, batch, ctx_len):
    ws, bs, cw, sink, wout, n2g, rwt, rb = wts
    tt = TT_MIX
    n_seq_tiles = seq // tt
    n_lat_tiles = n_lat // tt
    n_rows = p_all.shape[0] if has_ctx else n_lat
    n_tiles = n_rows // tt
    bpt = tt // BLOCK
    bps = seq // BLOCK

    def lat(t):
        tl = jnp.minimum(t, n_lat_tiles - 1)
        return tl, tl // n_seq_tiles

    def prev_map(t):
        tl, b = lat(t)
        return (jnp.maximum(tl * bpt - 1, b * bps), 0)

    def next_map(t):
        tl, b = lat(t)
        return (jnp.minimum((tl + 1) * bpt, (b + 1) * bps - 1), 0)

    row_map = lambda t: (t, 0)
    full = lambda *shape: pl.BlockSpec(shape, lambda t: (0,) * len(shape))
    in_specs = [
        pl.BlockSpec((tt, P_COLS), row_map),
        pl.BlockSpec((BLOCK, HALO_COLS), prev_map),
        pl.BlockSpec((BLOCK, HALO_COLS), next_map),
        pl.BlockSpec((ctx_len, HALO_COLS), lambda t: (n_lat // ctx_len + lat(t)[1], 0)),
        pl.BlockSpec((tt, D_MODEL), row_map),
        pl.BlockSpec((None, None, 6, D_MODEL),
                     lambda t: (layer, jnp.where(t < n_lat_tiles, lat(t)[1], batch), 0, 0)),
        full(A_GROUPS, A_CHUNK, A_CHUNK), full(A_CHUNK, W_A), full(8, W_B),
        pl.BlockSpec(memory_space=pltpu.SMEM),
        full(D_MODEL, D_MODEL), full(1, D_MODEL), full(N_EXPERTS, D_MODEL), full(N_EXPERTS, 1),
    ]
    out_shape = [jax.ShapeDtypeStruct((n_rows, D_MODEL), F32),
                 jax.ShapeDtypeStruct((n_rows, D_MODEL), F32),
                 jax.ShapeDtypeStruct((n_tiles, 8, tt), F32)]
    out_specs = [pl.BlockSpec((tt, D_MODEL), row_map), pl.BlockSpec((tt, D_MODEL), row_map),
                 pl.BlockSpec((None, 8, tt), lambda t: (t, 0, 0))]
    return pl.pallas_call(
        functools.partial(_mixer_body, tt, n_seq_tiles, n_lat_tiles, ctx_len, has_ctx),
        grid=(n_tiles,), in_specs=in_specs, out_specs=out_specs, out_shape=out_shape,
        scratch_shapes=[pltpu.VMEM((tt, D_MODEL), BF16)],
        compiler_params=_cparams(1),
        name="mixer",
    )(p_all, p_all, p_all, p_all, x_all, mod4, ws, bs, cw, sink, wout, n2g, rwt, rb)


def _moe_body(tm, e1_ref, e2_ref, nv_ref, tok_ref, h_hbm, wrow_ref,
              wg1, wu1, wd1, wg2, wu2, wd2, y_hbm, hbuf, ybuf, sem_g, sem_s):
    nv = nv_ref[pl.program_id(0)]

    @pl.when(nv > 0)
    def _():
        def gather(r, carry):
            t = tok_ref[0, 0, r]
            pltpu.make_async_copy(h_hbm.at[pl.ds(t, 1)], hbuf.at[pl.ds(r, 1)], sem_g).start()
            return carry

        lax.fori_loop(0, tm, gather, 0, unroll=8)
        pltpu.make_async_copy(h_hbm.at[pl.ds(0, tm)], hbuf, sem_g).wait()

        h = hbuf[...].astype(BF16)
        w = wrow_ref[...]

        def expert(wg, wu, wd):
            g = jnp.dot(h, wg[...], preferred_element_type=F32)
            u = jnp.dot(h, wu[...], preferred_element_type=F32)
            a = (jax.nn.silu(g) * u).astype(BF16)
            return jnp.dot(a, wd[...], preferred_element_type=F32)

        ybuf[...] = w[:, 0:1] * expert(wg1, wu1, wd1) + w[:, 1:2] * expert(wg2, wu2, wd2)

        def scatter(r, carry):
            t = tok_ref[0, 0, r]
            pltpu.make_async_copy(ybuf.at[pl.ds(r, 1)], y_hbm.at[pl.ds(t, 1)], sem_s).start()
            return carry

        lax.fori_loop(0, nv, scatter, 0)
        n_grp = pl.multiple_of((nv // SUBLANES) * SUBLANES, SUBLANES)

        @pl.when(n_grp > 0)
        def _():
            pltpu.make_async_copy(
                ybuf.at[pl.ds(0, n_grp)], y_hbm.at[pl.ds(0, n_grp)], sem_s).wait()

        def wait_row(r, carry):
            pltpu.make_async_copy(ybuf.at[pl.ds(0, 1)], y_hbm.at[pl.ds(0, 1)], sem_s).wait()
            return carry

        lax.fori_loop(n_grp, nv, wait_row, 0)


def _route_plan(cls, w, n_tok, tm):
    n_tiles = n_tok // tm + N_CLASSES
    order = jnp.argsort(cls, stable=True).astype(jnp.int32)
    counts = jnp.sum((cls[:, None] == jnp.arange(N_CLASSES)[None, :]).astype(jnp.int32), axis=0)
    tiles_per = (counts + tm - 1) // tm
    tile_end = jnp.cumsum(tiles_per)
    tile_start = tile_end - tiles_per
    tok_start = jnp.cumsum(counts) - counts
    ti = jnp.arange(n_tiles, dtype=jnp.int32)
    used = ti < tile_end[-1]
    tcls = jnp.minimum(jnp.searchsorted(tile_end, ti, side="right"), N_CLASSES - 1)
    tcls = jnp.where(used, tcls, tcls[jnp.maximum(tile_end[-1] - 1, 0)])
    off = (ti - tile_start[tcls]) * tm
    nv = jnp.where(used, jnp.clip(counts[tcls] - off, 0, tm), 0).astype(jnp.int32)
    r = jnp.arange(tm, dtype=jnp.int32)[None, :]
    ok = r < nv[:, None]
    pos = jnp.clip(tok_start[tcls][:, None] + off[:, None] + r, 0, n_tok - 1)
    tok = jnp.where(ok, order[pos], 0).astype(jnp.int32)
    wrow = jnp.where(ok[..., None], w[tok], 0.0).reshape(n_tiles * tm, 2)
    grp = tcls // N_PAIRS
    pair = jnp.asarray(PAIRS, jnp.int32)[tcls % N_PAIRS]
    e1 = (grp * EXPERTS_PER_GROUP + pair[:, 0]).astype(jnp.int32)
    e2 = (grp * EXPERTS_PER_GROUP + pair[:, 1]).astype(jnp.int32)
    return e1, e2, nv, tok.reshape(n_tiles, 1, tm), wrow


def _moe(h2_all, plan, wgate, wup, wdown, n_tok):
    tm = TM_MOE
    e1, e2, nv, tok, wrow = plan
    n_tiles = tok.shape[0]
    tok_spec = pl.BlockSpec((1, 1, tm), lambda i, *_: (i, 0, 0), memory_space=pltpu.SMEM)
    wsel = lambda shape, which: pl.BlockSpec(
        (None,) + shape, lambda i, e1r, e2r, nvr: ((e1r, e2r)[which][i], 0, 0))
    grid_spec = pltpu.PrefetchScalarGridSpec(
        num_scalar_prefetch=3,
        grid=(n_tiles,),
        in_specs=[
            tok_spec,
            pl.BlockSpec(memory_space=pl.ANY),
            pl.BlockSpec((tm, 2), lambda i, *_: (i, 0)),
            wsel((D_MODEL, D_EXPERT), 0), wsel((D_MODEL, D_EXPERT), 0), wsel((D_EXPERT, D_MODEL), 0),
            wsel((D_MODEL, D_EXPERT), 1), wsel((D_MODEL, D_EXPERT), 1), wsel((D_EXPERT, D_MODEL), 1),
        ],
        out_specs=pl.BlockSpec(memory_space=pl.ANY),
        scratch_shapes=[
            pltpu.VMEM((tm, D_MODEL), F32), pltpu.VMEM((tm, D_MODEL), F32),
            pltpu.SemaphoreType.DMA(()), pltpu.SemaphoreType.DMA(()),
        ],
    )
    return pl.pallas_call(
        functools.partial(_moe_body, tm),
        grid_spec=grid_spec,
        out_shape=jax.ShapeDtypeStruct((n_tok, D_MODEL), F32),
        compiler_params=_cparams(1),
        name="moe",
    )(e1, e2, nv, tok, h2_all, wrow, wgate, wup, wdown, wgate, wup, wdown)


def _final_body(x_ref, y_ref, mod_ref, g_ref, o_ref):
    x = x_ref[...] + mod_ref[5:6, :] * y_ref[...]
    o_ref[...] = _rms(x, g_ref[...])


def _final(layer, x_all, y_all, mod4, final_g, n_lat, seq):
    tt = TT_PROJ
    tiles_per_batch = seq // tt
    row_spec = pl.BlockSpec((tt, D_MODEL), lambda i: (i, 0))
    return pl.pallas_call(
        _final_body,
        grid=(n_lat // tt,),
        in_specs=[row_spec, row_spec,
                  pl.BlockSpec((None, None, 6, D_MODEL), lambda i: (layer, i // tiles_per_batch, 0, 0)),
                  pl.BlockSpec((1, D_MODEL), lambda i: (0, 0))],
        out_specs=row_spec,
        out_shape=jax.ShapeDtypeStruct((n_lat, D_MODEL), F32),
        compiler_params=_cparams(1),
        name="final_norm",
    )(x_all, y_all, mod4, final_g)


def _rope_tables(seq, pad_rows):
    rows = seq // GRID_W
    row = jnp.repeat(jnp.arange(rows, dtype=F32), GRID_W)
    col = jnp.tile(jnp.arange(GRID_W, dtype=F32), rows)
    inv = ROPE_BASE ** (-jnp.arange(QUARTER, dtype=F32) / QUARTER)
    ang = jnp.stack([row[:, None] * inv, col[:, None] * inv], axis=1)
    cos, sin = jnp.cos(ang), jnp.sin(ang)
    zero = jnp.zeros_like(sin)
    c64 = jnp.stack([cos, cos], axis=2).reshape(seq, HEAD_DIM)
    s1 = jnp.stack([-sin, zero], axis=2).reshape(seq, HEAD_DIM)
    s2 = jnp.stack([zero, sin], axis=2).reshape(seq, HEAD_DIM)
    out = []
    for t, fill in ((c64, 1.0), (s1, 0.0), (s2, 0.0)):
        t = jnp.concatenate([t, t], axis=1)
        out.append(jnp.concatenate([t, jnp.full((pad_rows, LANES), fill, F32)], axis=0))
    return out


def _permute_w_in(w_in):
    return jnp.concatenate(
        [w_in[..., 768:1024], w_in[..., 1024:1280], w_in[..., 1792:1920], w_in[..., 1920:2048],
         w_in[..., 0:256], w_in[..., 256:512], w_in[..., 512:768], w_in[..., 1280:1792]], axis=-1)


def kernel(x, c, ctx, c_ctx, w_mod, b_mod, norm1_g, norm2_g, w_in, gmlp_g, w_s, b_s, conv_w,
           attn_sink, w_out, router_w, router_b, w_gate, w_up, w_down, final_g):
    batch, seq, d = x.shape
    ctx_len = ctx.shape[1]
    depth = w_mod.shape[0]
    n_lat = batch * seq
    n_ctx = batch * ctx_len
    assert d == D_MODEL and seq % TT_MIX == 0 and n_ctx % TT_PROJ == 0 and n_ctx % TT_MIX == 0
    assert TT_MIX % ctx_len == 0 and ctx_len % BLOCK == 0 and n_lat % ctx_len == 0
    assert batch < MOD_ROWS

    c_all = jnp.concatenate(
        [c, c_ctx[None], jnp.zeros((MOD_ROWS - batch - 1, d), F32)], axis=0)
    mod4 = _modulation(c_all, w_mod, b_mod).reshape(depth, MOD_ROWS, 6, d)

    rope_tabs = _rope_tables(seq, TT_PROJ)
    win_b = _permute_w_in(w_in).astype(BF16)
    wout_b = w_out.astype(BF16)
    ws_b = w_s.astype(BF16)
    wg_b, wu_b, wd_b = w_gate.astype(BF16), w_up.astype(BF16), w_down.astype(BF16)
    rwt = router_w.T.astype(BF16)
    rb = router_b.reshape(N_EXPERTS, 1).astype(F32)
    bs_full = jnp.repeat(jnp.swapaxes(b_s, 1, 2), W_A // A_GROUPS, axis=2)
    cw_pad = jnp.concatenate([conv_w, jnp.zeros((depth, 5, W_B), F32)], axis=1)

    x_all = jnp.concatenate([x.reshape(n_lat, d), ctx.reshape(n_ctx, d)], axis=0)
    y_all = None
    for l in range(depth):
        last = l == depth - 1
        x_all, p_all = _inproj(l, x_all, y_all, mod4, norm1_g[l][None], gmlp_g[l][None], win_b[l],
                               rope_tabs, n_lat, seq, batch)
        wts = (ws_b[l], bs_full[l], cw_pad[l], attn_sink[l], wout_b[l], norm2_g[l][None], rwt, rb)
        x_all, h2, ro = _mixer(l, not last, p_all, x_all, mod4, wts, n_lat, seq, batch, ctx_len)
        n_tok = h2.shape[0]
        cls = ro[:, 0, :].reshape(-1).astype(jnp.int32)
        wpair = jnp.swapaxes(ro[:, 1:3, :], 1, 2).reshape(-1, 2)
        plan = _route_plan(cls, wpair, n_tok, TM_MOE)
        y_all = _moe(h2, plan, wg_b[l], wu_b[l], wd_b[l], n_tok)
    out = _final(depth - 1, x_all, y_all, mod4, final_g[None], n_lat, seq)
    return out.reshape(batch, seq, d)
```

```python
import functools

import jax
import jax.numpy as jnp
from jax import lax
from jax.experimental import pallas as pl
from jax.experimental.pallas import tpu as pltpu

F32 = jnp.float32
BF16 = jnp.bfloat16

D_MODEL = 1024
W_A = 256
A_GROUPS = 4
A_CHUNK = 128
W_B = 256
HEAD_DIM = 64
W_C = 512
N_HEADS = 8
N_KV = 2
BLOCK = 128
GRID_W = 64
QUARTER = 16
ROPE_BASE = 10000.0
ATTN_SCALE = HEAD_DIM ** -0.5
EPS = 1e-6
N_EXPERTS = 16
N_GROUPS = 4
EXPERTS_PER_GROUP = 4
D_EXPERT = 512
N_PAIRS = 6
N_CLASSES = N_GROUPS * N_PAIRS
PAIRS = ((0, 1), (0, 2), (0, 3), (1, 2), (1, 3), (2, 3))

LANES = 128
SUBLANES = 8
MOD_ROWS = 16
PROJ_COLS = 2048
P_Z, P_K, P_V, P_U, P_GV, P_BG, P_Q = 0, 256, 384, 512, 768, 1024, 1280
P_COLS = 1792
HALO_COLS = 512
H2_COLS = D_MODEL + LANES

TT_PROJ = 512
TT_MIX = 512
TM_MOE = 256
VMEM_LIMIT = 52 * 1024 * 1024


def _cparams(n_axes):
    return pltpu.CompilerParams(
        dimension_semantics=("arbitrary",) * n_axes, vmem_limit_bytes=VMEM_LIMIT)


def _mod_body(c_ref, w_ref, b_ref, o_ref):
    a = jax.nn.silu(c_ref[...]).astype(BF16)
    o_ref[...] = jnp.dot(a, w_ref[...].astype(BF16), preferred_element_type=F32) + b_ref[...]


def _modulation(c_all, w_mod, b_mod):
    depth = w_mod.shape[0]
    tn = 1536
    return pl.pallas_call(
        _mod_body,
        grid=(depth, 6 * D_MODEL // tn),
        in_specs=[
            pl.BlockSpec((MOD_ROWS, D_MODEL), lambda l, n: (0, 0)),
            pl.BlockSpec((None, D_MODEL, tn), lambda l, n: (l, 0, n)),
            pl.BlockSpec((None, 1, tn), lambda l, n: (l, 0, n)),
        ],
        out_specs=pl.BlockSpec((None, MOD_ROWS, tn), lambda l, n: (l, 0, n)),
        out_shape=jax.ShapeDtypeStruct((depth, MOD_ROWS, 6 * D_MODEL), F32),
        compiler_params=_cparams(2),
        name="modulation",
    )(c_all, w_mod, b_mod.reshape(depth, 1, 6 * D_MODEL))


def _rms(x, g):
    return x * lax.rsqrt(jnp.mean(x * x, axis=-1, keepdims=True) + EPS) * g


def _rope(t, c, s1, s2):
    return t * c + pltpu.roll(t, LANES - QUARTER, axis=1) * s1 + pltpu.roll(t, QUARTER, axis=1) * s2


def _inproj_body(has_y, *refs):
    if has_y:
        (x_ref, y_ref, mod_ref, modp_ref, n1g_ref, gg_ref, win_ref, c_ref, s1_ref, s2_ref,
         xo_ref, p_ref) = refs
        x = x_ref[...] + modp_ref[5:6, :] * y_ref[...]
        xo_ref[...] = x
    else:
        x_ref, mod_ref, n1g_ref, gg_ref, win_ref, c_ref, s1_ref, s2_ref, p_ref = refs
        x = x_ref[...]
    hx = _rms(x, n1g_ref[...]) * (1.0 + mod_ref[1:2, :]) + mod_ref[0:1, :]
    hb = hx.astype(BF16)

    def proj(lo, hi):
        return jnp.dot(hb, win_ref[:, lo:hi], preferred_element_type=F32)

    c, s1, s2 = c_ref[...], s1_ref[...], s2_ref[...]
    p_ref[:, P_Z:P_Z + W_B] = (proj(0, 256) * proj(256, 512)).astype(BF16)
    p_ref[:, P_K:P_K + LANES] = _rope(proj(512, 640), c, s1, s2).astype(BF16)
    p_ref[:, P_V:P_V + LANES] = proj(640, 768).astype(BF16)
    p_ref[:, P_U:P_U + W_A] = jax.nn.gelu(proj(768, 1024)).astype(BF16)
    p_ref[:, P_GV:P_GV + W_A] = _rms(jax.nn.gelu(proj(1024, 1280)), gg_ref[...]).astype(BF16)
    p_ref[:, P_BG:P_BG + W_B] = proj(1280, 1536).astype(BF16)
    for i in range(W_C // LANES):
        q = _rope(proj(1536 + i * LANES, 1536 + (i + 1) * LANES), c, s1, s2) * ATTN_SCALE
        p_ref[:, P_Q + i * LANES:P_Q + (i + 1) * LANES] = q.astype(BF16)


def _inproj(layer, x_all, y_all, mod4, n1g, gg, win, rope_tabs, n_lat, seq, batch):
    n_all = x_all.shape[0]
    tt = TT_PROJ
    tiles_per_batch = seq // tt
    n_lat_tiles = n_lat // tt
    has_y = y_all is not None

    def mod_row(i):
        return jnp.minimum(i // tiles_per_batch, batch)

    def rope_blk(i):
        return jnp.where(i < n_lat_tiles, i % tiles_per_batch, tiles_per_batch)

    row_spec = pl.BlockSpec((tt, D_MODEL), lambda i: (i, 0))
    in_specs = [row_spec]
    args = [x_all]
    if has_y:
        in_specs.append(row_spec)
        args.append(y_all)
    in_specs.append(pl.BlockSpec((None, None, 6, D_MODEL), lambda i: (layer, mod_row(i), 0, 0)))
    args.append(mod4)
    if has_y:
        in_specs.append(
            pl.BlockSpec((None, None, 6, D_MODEL), lambda i: (layer - 1, mod_row(i), 0, 0)))
        args.append(mod4)
    in_specs += [
        pl.BlockSpec((1, D_MODEL), lambda i: (0, 0)),
        pl.BlockSpec((1, W_A), lambda i: (0, 0)),
        pl.BlockSpec((D_MODEL, PROJ_COLS), lambda i: (0, 0)),
    ]
    args += [n1g, gg, win]
    for tab in rope_tabs:
        in_specs.append(pl.BlockSpec((tt, LANES), lambda i: (rope_blk(i), 0)))
        args.append(tab)
    p_shape = jax.ShapeDtypeStruct((n_all, P_COLS), BF16)
    p_spec = pl.BlockSpec((tt, P_COLS), lambda i: (i, 0))
    if has_y:
        out_shape = (jax.ShapeDtypeStruct((n_all, D_MODEL), F32), p_shape)
        out_specs = (row_spec, p_spec)
    else:
        out_shape, out_specs = p_shape, p_spec
    out = pl.pallas_call(
        functools.partial(_inproj_body, has_y),
        grid=(n_all // tt,),
        in_specs=in_specs, out_specs=out_specs, out_shape=out_shape,
        compiler_params=_cparams(1),
        name="inproj",
    )(*args)
    return out if has_y else (x_all, out)


def _route(logits_t, rb):
    sc = jax.nn.sigmoid(logits_t)
    sel = sc + rb
    sel_r = [sel[e:e + 1, :] for e in range(N_EXPERTS)]
    sc_r = [sc[e:e + 1, :] for e in range(N_EXPERTS)]
    best = None
    for g in range(N_GROUPS):
        v = sel_r[g * 4:g * 4 + 4]
        gs = None
        for (i, j) in PAIRS:
            pair = v[i] + v[j]
            gs = pair if gs is None else jnp.maximum(gs, pair)
        if best is None:
            best_score, best = gs, jnp.zeros_like(gs)
            vb = list(v)
            sb = list(sc_r[0:4])
        else:
            better = gs > best_score
            best_score = jnp.where(better, gs, best_score)
            best = jnp.where(better, float(g), best)
            vb = [jnp.where(better, v[i], vb[i]) for i in range(4)]
            sb = [jnp.where(better, sc_r[g * 4 + i], sb[i]) for i in range(4)]
    chosen = []
    for i in range(4):
        rank = jnp.zeros_like(best)
        for j in range(4):
            if j == i:
                continue
            beats = (vb[j] >= vb[i]) if j < i else (vb[j] > vb[i])
            rank = rank + jnp.where(beats, 1.0, 0.0)
        chosen.append(rank < 2.0)
    pair_id = jnp.zeros_like(best)
    w_lo = jnp.zeros_like(best)
    w_hi = jnp.zeros_like(best)
    for p, (i, j) in enumerate(PAIRS):
        hit = chosen[i] & chosen[j]
        tot = sb[i] + sb[j]
        pair_id = jnp.where(hit, float(p), pair_id)
        w_lo = jnp.where(hit, sb[i] / tot, w_lo)
        w_hi = jnp.where(hit, sb[j] / tot, w_hi)
    return best * float(N_PAIRS) + pair_id, w_lo, w_hi


def _gmlp_rows(p_ref, ws_ref, bs_ref, mix_ref, row0, n_rows):
    lane_grp = lax.broadcasted_iota(jnp.int32, (A_CHUNK, W_A), 1) // (W_A // A_GROUPS)
    for c in range(n_rows // A_CHUNK):
        rows = slice(row0 + c * A_CHUNK, row0 + (c + 1) * A_CHUNK)
        vch = p_ref[rows, P_GV:P_GV + W_A]
        s = bs_ref[...]
        for h in range(A_GROUPS):
            sh = jnp.dot(ws_ref[h], vch, preferred_element_type=F32)
            s = s + jnp.where(lane_grp == h, sh, 0.0)
        mix_ref[rows, 0:W_A] = (p_ref[rows, P_U:P_U + W_A].astype(F32) * s).astype(BF16)


def _conv_rows(p_ref, cw_ref, mix_ref, row0, n_rows, prev_row, next_row):
    rows = slice(row0, row0 + n_rows)
    z = p_ref[rows, P_Z:P_Z + W_B].astype(F32)
    ridx = lax.broadcasted_iota(jnp.int32, (n_rows, W_B), 0)
    z_up = jnp.where(ridx == 0, prev_row, pltpu.roll(z, 1, axis=0))
    z_dn = jnp.where(ridx == n_rows - 1, next_row, pltpu.roll(z, n_rows - 1, axis=0))
    conv = z_up * cw_ref[0:1, :] + z * cw_ref[1:2, :] + z_dn * cw_ref[2:3, :]
    mix_ref[rows, W_A:W_A + W_B] = (p_ref[rows, P_BG:P_BG + W_B].astype(F32) * conv).astype(BF16)


def _dup_heads(t):
    lane_lo = lax.broadcasted_iota(jnp.int32, (1, LANES), 1) < HEAD_DIM
    tf = t.astype(F32)
    tsw = pltpu.roll(tf, HEAD_DIM, axis=1)
    return (jnp.where(lane_lo, tf, tsw).astype(BF16), jnp.where(lane_lo, tsw, tf).astype(BF16))


def _attend_rows(p_ref, sink_ref, mix_ref, row0, n_qblocks, kc_d, vc_d, local):
    lane_lo = lax.broadcasted_iota(jnp.int32, (1, LANES), 1) < HEAD_DIM
    row_even = lax.broadcasted_iota(jnp.int32, (2 * BLOCK, 1), 0) < BLOCK
    zero_q = jnp.zeros((BLOCK, LANES), BF16)
    nt_dims = (((1,), (1,)), ((), ()))
    n_loc = 3 * BLOCK
    if local is not None:
        kl_d, vl_d, first_tile, last_tile = local
        col = lax.broadcasted_iota(jnp.int32, (2 * BLOCK, n_loc), 1)
        qrow = lax.broadcasted_iota(jnp.int32, (2 * BLOCK, n_loc), 0) % BLOCK
    for qb in range(n_qblocks):
        rows = slice(row0 + qb * BLOCK, row0 + (qb + 1) * BLOCK)
        if local is not None:
            lo_b = jnp.where(first_tile & (qb == 0), BLOCK, 0)
            hi_b = jnp.where(last_tile & (qb == n_qblocks - 1), 2 * BLOCK, 3 * BLOCK)
            valid = (col >= jnp.maximum(qrow, lo_b)) & (col <= jnp.minimum(qrow + 2 * BLOCK, hi_b - 1))
        for i in range(W_C // LANES):
            kv = i // 2
            q = p_ref[rows, P_Q + i * LANES:P_Q + (i + 1) * LANES]
            qz = jnp.concatenate([jnp.where(lane_lo, q, zero_q), jnp.where(lane_lo, zero_q, q)], 0)
            sk = jnp.where(row_even, sink_ref[2 * i], sink_ref[2 * i + 1])
            s_ctx = lax.dot_general(qz, kc_d[kv], nt_dims, preferred_element_type=F32)
            m = jnp.maximum(jnp.max(s_ctx, axis=1, keepdims=True), sk)
            if local is not None:
                k_loc = kl_d[kv][qb * BLOCK:qb * BLOCK + n_loc]
                s_loc = lax.dot_general(qz, k_loc, nt_dims, preferred_element_type=F32)
                s_loc = jnp.where(valid, s_loc, -jnp.inf)
                m = jnp.maximum(m, jnp.max(s_loc, axis=1, keepdims=True))
            p_ctx = jnp.exp(s_ctx - m)
            den = jnp.sum(p_ctx, axis=1, keepdims=True) + jnp.exp(sk - m)
            o = jnp.dot(p_ctx.astype(BF16), vc_d[kv], preferred_element_type=F32)
            if local is not None:
                p_loc = jnp.exp(s_loc - m)
                den = den + jnp.sum(p_loc, axis=1, keepdims=True)
                v_loc = vl_d[kv][qb * BLOCK:qb * BLOCK + n_loc]
                o = o + jnp.dot(p_loc.astype(BF16), v_loc, preferred_element_type=F32)
            o = o / den
            slab = jnp.where(lane_lo, o[0:BLOCK], o[BLOCK:2 * BLOCK])
            c0 = W_A + W_B + i * LANES
            mix_ref[rows, c0:c0 + LANES] = slab.astype(BF16)


def _mixer_body(tt, n_seq_tiles, n_lat_tiles, ctx_len, has_ctx,
                p_ref, hp_ref, hn_ref, pc_ref, x_ref, mod_ref, ws_ref, bs_ref, cw_ref, sink_ref,
                wout_ref, n2g_ref, rwt_ref, rb_ref, xo_ref, h2_ref, ro_ref, mix_ref):
    t = pl.program_id(0)

    def latent_tile():
        j = t % n_seq_tiles
        first_tile, last_tile = j == 0, j == n_seq_tiles - 1
        _gmlp_rows(p_ref, ws_ref, bs_ref, mix_ref, 0, tt)
        prev_row = hp_ref[BLOCK - 16:BLOCK, P_Z:P_Z + W_B].astype(F32)[15:16, :]
        next_row = hn_ref[0:16, P_Z:P_Z + W_B].astype(F32)[0:1, :]
        _conv_rows(p_ref, cw_ref, mix_ref, 0, tt,
                   jnp.where(first_tile, 0.0, prev_row), jnp.where(last_tile, 0.0, next_row))
        k_ext = jnp.concatenate(
            [hp_ref[:, P_K:P_K + LANES], p_ref[:, P_K:P_K + LANES], hn_ref[:, P_K:P_K + LANES]], 0)
        v_ext = jnp.concatenate(
            [hp_ref[:, P_V:P_V + LANES], p_ref[:, P_V:P_V + LANES], hn_ref[:, P_V:P_V + LANES]], 0)
        _attend_rows(p_ref, sink_ref, mix_ref, 0, tt // BLOCK,
                     _dup_heads(pc_ref[:, P_K:P_K + LANES]), _dup_heads(pc_ref[:, P_V:P_V + LANES]),
                     (_dup_heads(k_ext), _dup_heads(v_ext), first_tile, last_tile))

    def context_tile():
        zero_row = jnp.zeros((1, W_B), F32)
        for seg in range(tt // ctx_len):
            r0 = seg * ctx_len
            _gmlp_rows(p_ref, ws_ref, bs_ref, mix_ref, r0, ctx_len)
            _conv_rows(p_ref, cw_ref, mix_ref, r0, ctx_len, zero_row, zero_row)
            _attend_rows(p_ref, sink_ref, mix_ref, r0, ctx_len // BLOCK,
                         _dup_heads(p_ref[r0:r0 + ctx_len, P_K:P_K + LANES]),
                         _dup_heads(p_ref[r0:r0 + ctx_len, P_V:P_V + LANES]), None)

    if has_ctx:
        pl.when(t < n_lat_tiles)(latent_tile)
        pl.when(t >= n_lat_tiles)(context_tile)
    else:
        latent_tile()

    mixed = jnp.dot(mix_ref[...], wout_ref[...], preferred_element_type=F32)
    x = x_ref[...] + mod_ref[2:3, :] * mixed
    xo_ref[...] = x
    h2 = _rms(x, n2g_ref[...]) * (1.0 + mod_ref[4:5, :]) + mod_ref[3:4, :]
    h2_ref[:, 0:D_MODEL] = h2
    logits_t = lax.dot_general(rwt_ref[...], h2.astype(BF16), (((1,), (1,)), ((), ())),
                               preferred_element_type=F32)
    cls, w_lo, w_hi = _route(logits_t, rb_ref[...])
    rid = lax.broadcasted_iota(jnp.int32, (SUBLANES, tt), 0)
    ro = jnp.where(rid == 0, cls, jnp.where(rid == 1, w_lo, jnp.where(rid == 2, w_hi, 0.0)))
    ro_ref[...] = ro
    ro_pad = jnp.concatenate([ro, jnp.zeros((LANES - SUBLANES, tt), F32)], axis=0)
    for c in range(tt // LANES):
        h2_ref[c * LANES:(c + 1) * LANES, D_MODEL:H2_COLS] = ro_pad[:, c * LANES:(c + 1) * LANES].T


def _mixer(layer, has_ctx, p_all, x_all, mod4, wts, n_lat, seq, batch, ctx_len):
    ws, bs, cw, sink, wout, n2g, rwt, rb = wts
    tt = TT_MIX
    n_seq_tiles = seq // tt
    n_lat_tiles = n_lat // tt
    n_rows = p_all.shape[0] if has_ctx else n_lat
    n_tiles = n_rows // tt
    bpt = tt // BLOCK
    bps = seq // BLOCK

    def lat(t):
        tl = jnp.minimum(t, n_lat_tiles - 1)
        return tl, tl // n_seq_tiles

    def prev_map(t):
        tl, b = lat(t)
        return (jnp.maximum(tl * bpt - 1, b * bps), 0)

    def next_map(t):
        tl, b = lat(t)
        return (jnp.minimum((tl + 1) * bpt, (b + 1) * bps - 1), 0)

    row_map = lambda t: (t, 0)
    full = lambda *shape: pl.BlockSpec(shape, lambda t: (0,) * len(shape))
    in_specs = [
        pl.BlockSpec((tt, P_COLS), row_map),
        pl.BlockSpec((BLOCK, HALO_COLS), prev_map),
        pl.BlockSpec((BLOCK, HALO_COLS), next_map),
        pl.BlockSpec((ctx_len, HALO_COLS), lambda t: (n_lat // ctx_len + lat(t)[1], 0)),
        pl.BlockSpec((tt, D_MODEL), row_map),
        pl.BlockSpec((None, None, 6, D_MODEL),
                     lambda t: (layer, jnp.where(t < n_lat_tiles, lat(t)[1], batch), 0, 0)),
        full(A_GROUPS, A_CHUNK, A_CHUNK), full(A_CHUNK, W_A), full(8, W_B),
        pl.BlockSpec(memory_space=pltpu.SMEM),
        full(D_MODEL, D_MODEL), full(1, D_MODEL), full(N_EXPERTS, D_MODEL), full(N_EXPERTS, 1),
    ]
    out_shape = [jax.ShapeDtypeStruct((n_rows, D_MODEL), F32),
                 jax.ShapeDtypeStruct((n_rows, H2_COLS), F32),
                 jax.ShapeDtypeStruct((n_tiles, SUBLANES, tt), F32)]
    out_specs = [pl.BlockSpec((tt, D_MODEL), row_map), pl.BlockSpec((tt, H2_COLS), row_map),
                 pl.BlockSpec((None, SUBLANES, tt), lambda t: (t, 0, 0))]
    return pl.pallas_call(
        functools.partial(_mixer_body, tt, n_seq_tiles, n_lat_tiles, ctx_len, has_ctx),
        grid=(n_tiles,), in_specs=in_specs, out_specs=out_specs, out_shape=out_shape,
        scratch_shapes=[pltpu.VMEM((tt, D_MODEL), BF16)],
        compiler_params=_cparams(1),
        name="mixer",
    )(p_all, p_all, p_all, p_all, x_all, mod4, ws, bs, cw, sink, wout, n2g, rwt, rb)


def _moe_body(tm, n_tiles, e1_ref, e2_ref, nv_ref, tok_ref, tokn_ref, h_hbm,
              wg1, wu1, wd1, wg2, wu2, wd2, y_hbm, hbuf, ybuf, sem_g, sem_s):
    i = pl.program_id(0)
    nv = nv_ref[i]
    slot = i % 2

    def start_gather(idx_ref, dst_slot):
        def body(r, carry):
            t = idx_ref[0, 0, r]
            pltpu.make_async_copy(
                h_hbm.at[pl.ds(t, 1)], hbuf.at[dst_slot, pl.ds(r, 1)], sem_g.at[dst_slot]).start()
            return carry

        lax.fori_loop(0, tm, body, 0, unroll=8)

    def wait_scatter(src_slot, n_rows):
        n_grp = pl.multiple_of((n_rows // SUBLANES) * SUBLANES, SUBLANES)

        @pl.when(n_grp > 0)
        def _():
            pltpu.make_async_copy(ybuf.at[src_slot, pl.ds(0, n_grp)], y_hbm.at[pl.ds(0, n_grp)],
                                  sem_s.at[src_slot]).wait()

        def wait_row(r, carry):
            pltpu.make_async_copy(ybuf.at[src_slot, pl.ds(0, 1)], y_hbm.at[pl.ds(0, 1)],
                                  sem_s.at[src_slot]).wait()
            return carry

        lax.fori_loop(n_grp, n_rows, wait_row, 0)

    @pl.when((i == 0) & (nv > 0))
    def _():
        start_gather(tok_ref, 0)

    @pl.when(nv > 0)
    def _():
        nv_next = jnp.where(i + 1 < n_tiles, nv_ref[jnp.minimum(i + 1, n_tiles - 1)], 0)

        @pl.when(nv_next > 0)
        def _():
            start_gather(tokn_ref, 1 - slot)

        pltpu.make_async_copy(h_hbm.at[pl.ds(0, tm)], hbuf.at[slot], sem_g.at[slot]).wait()

        @pl.when(i >= 2)
        def _():
            wait_scatter(slot, nv_ref[jnp.maximum(i - 2, 0)])

        rows = hbuf[slot]
        h = rows[:, 0:D_MODEL].astype(BF16)
        w_lo = rows[:, D_MODEL + 1:D_MODEL + 2]
        w_hi = rows[:, D_MODEL + 2:D_MODEL + 3]

        def expert(wg, wu, wd):
            g = jnp.dot(h, wg[...], preferred_element_type=F32)
            u = jnp.dot(h, wu[...], preferred_element_type=F32)
            a = (jax.nn.silu(g) * u).astype(BF16)
            return jnp.dot(a, wd[...], preferred_element_type=F32)

        ybuf[slot] = w_lo * expert(wg1, wu1, wd1) + w_hi * expert(wg2, wu2, wd2)

        def scatter(r, carry):
            t = tok_ref[0, 0, r]
            pltpu.make_async_copy(
                ybuf.at[slot, pl.ds(r, 1)], y_hbm.at[pl.ds(t, 1)], sem_s.at[slot]).start()
            return carry

        lax.fori_loop(0, nv, scatter, 0)

        @pl.when(nv_next == 0)
        def _():
            @pl.when(i >= 1)
            def _():
                wait_scatter(1 - slot, nv_ref[jnp.maximum(i - 1, 0)])

            wait_scatter(slot, nv)


def _route_plan(cls, n_tok, tm):
    n_tiles = n_tok // tm + N_CLASSES
    order = jnp.argsort(cls, stable=True).astype(jnp.int32)
    counts = jnp.sum((cls[:, None] == jnp.arange(N_CLASSES)[None, :]).astype(jnp.int32), axis=0)
    tiles_per = (counts + tm - 1) // tm
    tile_end = jnp.cumsum(tiles_per)
    tile_start = tile_end - tiles_per
    tok_start = jnp.cumsum(counts) - counts
    ti = jnp.arange(n_tiles, dtype=jnp.int32)
    used = ti < tile_end[-1]
    last_used = jnp.maximum(tile_end[-1] - 1, 0)
    tcls = jnp.sum((jnp.minimum(ti, last_used)[:, None] >= tile_end[None, :]).astype(jnp.int32), axis=1)
    tcls = jnp.minimum(tcls, N_CLASSES - 1)
    off = (ti - tile_start[tcls]) * tm
    nv = jnp.where(used, jnp.clip(counts[tcls] - off, 0, tm), 0).astype(jnp.int32)
    r = jnp.arange(tm, dtype=jnp.int32)[None, :]
    ok = r < nv[:, None]
    pos = jnp.clip(tok_start[tcls][:, None] + off[:, None] + r, 0, n_tok - 1)
    tok = jnp.where(ok, order[pos], 0).astype(jnp.int32)
    grp = tcls // N_PAIRS
    pair = jnp.asarray(PAIRS, jnp.int32)[tcls % N_PAIRS]
    e1 = (grp * EXPERTS_PER_GROUP + pair[:, 0]).astype(jnp.int32)
    e2 = (grp * EXPERTS_PER_GROUP + pair[:, 1]).astype(jnp.int32)
    return e1, e2, nv, tok.reshape(n_tiles, 1, tm)


def _moe(h2_all, plan, wgate, wup, wdown, n_tok):
    tm = TM_MOE
    e1, e2, nv, tok = plan
    n_tiles = tok.shape[0]
    smem_blk = lambda fn: pl.BlockSpec((1, 1, tm), fn, memory_space=pltpu.SMEM)
    wsel = lambda shape, which: pl.BlockSpec(
        (None,) + shape, lambda i, e1r, e2r, nvr: ((e1r, e2r)[which][i], 0, 0))
    grid_spec = pltpu.PrefetchScalarGridSpec(
        num_scalar_prefetch=3,
        grid=(n_tiles,),
        in_specs=[
            smem_blk(lambda i, *_: (i, 0, 0)),
            smem_blk(lambda i, *_: (jnp.minimum(i + 1, n_tiles - 1), 0, 0)),
            pl.BlockSpec(memory_space=pl.ANY),
            wsel((D_MODEL, D_EXPERT), 0), wsel((D_MODEL, D_EXPERT), 0), wsel((D_EXPERT, D_MODEL), 0),
            wsel((D_MODEL, D_EXPERT), 1), wsel((D_MODEL, D_EXPERT), 1), wsel((D_EXPERT, D_MODEL), 1),
        ],
        out_specs=pl.BlockSpec(memory_space=pl.ANY),
        scratch_shapes=[
            pltpu.VMEM((2, tm, H2_COLS), F32), pltpu.VMEM((2, tm, D_MODEL), F32),
            pltpu.SemaphoreType.DMA((2,)), pltpu.SemaphoreType.DMA((2,)),
        ],
    )
    return pl.pallas_call(
        functools.partial(_moe_body, tm, n_tiles),
        grid_spec=grid_spec,
        out_shape=jax.ShapeDtypeStruct((n_tok, D_MODEL), F32),
        compiler_params=_cparams(1),
        name="moe",
    )(e1, e2, nv, tok, tok, h2_all, wgate, wup, wdown, wgate, wup, wdown)


def _final_body(x_ref, y_ref, mod_ref, g_ref, o_ref):
    x = x_ref[...] + mod_ref[5:6, :] * y_ref[...]
    o_ref[...] = _rms(x, g_ref[...])


def _final(layer, x_all, y_all, mod4, final_g, n_lat, seq):
    tt = TT_PROJ
    tiles_per_batch = seq // tt
    row_spec = pl.BlockSpec((tt, D_MODEL), lambda i: (i, 0))
    return pl.pallas_call(
        _final_body,
        grid=(n_lat // tt,),
        in_specs=[row_spec, row_spec,
                  pl.BlockSpec((None, None, 6, D_MODEL), lambda i: (layer, i // tiles_per_batch, 0, 0)),
                  pl.BlockSpec((1, D_MODEL), lambda i: (0, 0))],
        out_specs=row_spec,
        out_shape=jax.ShapeDtypeStruct((n_lat, D_MODEL), F32),
        compiler_params=_cparams(1),
        name="final_norm",
    )(x_all, y_all, mod4, final_g)


def _rope_tables(seq, pad_rows):
    rows = seq // GRID_W
    row = jnp.repeat(jnp.arange(rows, dtype=F32), GRID_W)
    col = jnp.tile(jnp.arange(GRID_W, dtype=F32), rows)
    inv = ROPE_BASE ** (-jnp.arange(QUARTER, dtype=F32) / QUARTER)
    ang = jnp.stack([row[:, None] * inv, col[:, None] * inv], axis=1)
    cos, sin = jnp.cos(ang), jnp.sin(ang)
    zero = jnp.zeros_like(sin)
    c64 = jnp.stack([cos, cos], axis=2).reshape(seq, HEAD_DIM)
    s1 = jnp.stack([-sin, zero], axis=2).reshape(seq, HEAD_DIM)
    s2 = jnp.stack([zero, sin], axis=2).reshape(seq, HEAD_DIM)
    out = []
    for t, fill in ((c64, 1.0), (s1, 0.0), (s2, 0.0)):
        t = jnp.concatenate([t, t], axis=1)
        out.append(jnp.concatenate([t, jnp.full((pad_rows, LANES), fill, F32)], axis=0))
    return out


def _permute_w_in(w_in):
    return jnp.concatenate(
        [w_in[..., 768:1024], w_in[..., 1024:1280], w_in[..., 1792:1920], w_in[..., 1920:2048],
         w_in[..., 0:256], w_in[..., 256:512], w_in[..., 512:768], w_in[..., 1280:1792]], axis=-1)


def kernel(x, c, ctx, c_ctx, w_mod, b_mod, norm1_g, norm2_g, w_in, gmlp_g, w_s, b_s, conv_w,
           attn_sink, w_out, router_w, router_b, w_gate, w_up, w_down, final_g):
    batch, seq, d = x.shape
    ctx_len = ctx.shape[1]
    depth = w_mod.shape[0]
    n_lat = batch * seq
    n_ctx = batch * ctx_len
    assert d == D_MODEL and seq % TT_MIX == 0 and n_ctx % TT_PROJ == 0 and n_ctx % TT_MIX == 0
    assert TT_MIX % ctx_len == 0 and ctx_len % BLOCK == 0 and n_lat % ctx_len == 0
    assert batch < MOD_ROWS

    c_all = jnp.concatenate(
        [c, c_ctx[None], jnp.zeros((MOD_ROWS - batch - 1, d), F32)], axis=0)
    mod4 = _modulation(c_all, w_mod, b_mod).reshape(depth, MOD_ROWS, 6, d)

    rope_tabs = _rope_tables(seq, TT_PROJ)
    win_b = _permute_w_in(w_in).astype(BF16)
    wout_b = w_out.astype(BF16)
    ws_b = w_s.astype(BF16)
    wg_b, wu_b, wd_b = w_gate.astype(BF16), w_up.astype(BF16), w_down.astype(BF16)
    rwt = router_w.T.astype(BF16)
    rb = router_b.reshape(N_EXPERTS, 1).astype(F32)
    bs_full = jnp.repeat(jnp.swapaxes(b_s, 1, 2), W_A // A_GROUPS, axis=2)
    cw_pad = jnp.concatenate([conv_w, jnp.zeros((depth, 5, W_B), F32)], axis=1)

    x_all = jnp.concatenate([x.reshape(n_lat, d), ctx.reshape(n_ctx, d)], axis=0)
    y_all = None
    for l in range(depth):
        last = l == depth - 1
        x_all, p_all = _inproj(l, x_all, y_all, mod4, norm1_g[l][None], gmlp_g[l][None], win_b[l],
                               rope_tabs, n_lat, seq, batch)
        wts = (ws_b[l], bs_full[l], cw_pad[l], attn_sink[l], wout_b[l], norm2_g[l][None], rwt, rb)
        x_all, h2, ro = _mixer(l, not last, p_all, x_all, mod4, wts, n_lat, seq, batch, ctx_len)
        n_tok = h2.shape[0]
        cls = ro[:, 0, :].reshape(-1).astype(jnp.int32)
        plan = _route_plan(cls, n_tok, TM_MOE)
        y_all = _moe(h2, plan, wg_b[l], wu_b[l], wd_b[l], n_tok)
    out = _final(depth - 1, x_all, y_all, mod4, final_g[None], n_lat, seq)
    return out.reshape(batch, seq, d)
```

```python
import functools

import jax
import jax.numpy as jnp
from jax import lax
from jax.experimental import pallas as pl
from jax.experimental.pallas import tpu as pltpu

F32 = jnp.float32
BF16 = jnp.bfloat16

D_MODEL = 1024
W_A = 256
A_GROUPS = 4
A_CHUNK = 128
W_B = 256
HEAD_DIM = 64
W_C = 512
N_HEADS = 8
N_KV = 2
BLOCK = 128
GRID_W = 64
QUARTER = 16
ROPE_BASE = 10000.0
ATTN_SCALE = HEAD_DIM ** -0.5
EPS = 1e-6
N_EXPERTS = 16
N_GROUPS = 4
EXPERTS_PER_GROUP = 4
D_EXPERT = 512
N_PAIRS = 6
N_CLASSES = N_GROUPS * N_PAIRS
PAIRS = ((0, 1), (0, 2), (0, 3), (1, 2), (1, 3), (2, 3))

LANES = 128
SUBLANES = 8
MOD_ROWS = 16
PROJ_COLS = 2048
P_Z, P_K, P_V, P_U, P_GV, P_BG, P_Q = 0, 256, 384, 512, 768, 1024, 1280
P_COLS = 1792
HALO_COLS = 512
H2_COLS = D_MODEL + LANES

TT_PROJ = 512
TT_MIX = 512
TM_MOE = 256
VMEM_LIMIT = 52 * 1024 * 1024


def _cparams(n_axes):
    return pltpu.CompilerParams(
        dimension_semantics=("arbitrary",) * n_axes, vmem_limit_bytes=VMEM_LIMIT)


def _mod_body(c_ref, w_ref, b_ref, o_ref):
    a = jax.nn.silu(c_ref[...]).astype(BF16)
    o_ref[...] = jnp.dot(a, w_ref[...].astype(BF16), preferred_element_type=F32) + b_ref[...]


def _modulation(c_all, w_mod, b_mod):
    depth = w_mod.shape[0]
    tn = 1536
    return pl.pallas_call(
        _mod_body,
        grid=(depth, 6 * D_MODEL // tn),
        in_specs=[
            pl.BlockSpec((MOD_ROWS, D_MODEL), lambda l, n: (0, 0)),
            pl.BlockSpec((None, D_MODEL, tn), lambda l, n: (l, 0, n)),
            pl.BlockSpec((None, 1, tn), lambda l, n: (l, 0, n)),
        ],
        out_specs=pl.BlockSpec((None, MOD_ROWS, tn), lambda l, n: (l, 0, n)),
        out_shape=jax.ShapeDtypeStruct((depth, MOD_ROWS, 6 * D_MODEL), F32),
        compiler_params=_cparams(2),
        name="modulation",
    )(c_all, w_mod, b_mod.reshape(depth, 1, 6 * D_MODEL))


def _rms(x, g):
    return x * lax.rsqrt(jnp.mean(x * x, axis=-1, keepdims=True) + EPS) * g


def _rope(t, c, s1, s2):
    return t * c + pltpu.roll(t, LANES - QUARTER, axis=1) * s1 + pltpu.roll(t, QUARTER, axis=1) * s2


def _inproj_body(has_y, *refs):
    if has_y:
        (x_ref, y_ref, mod_ref, modp_ref, n1g_ref, gg_ref, win_ref, c_ref, s1_ref, s2_ref,
         xo_ref, p_ref) = refs
        x = x_ref[...] + modp_ref[5:6, :] * y_ref[...]
        xo_ref[...] = x
    else:
        x_ref, mod_ref, n1g_ref, gg_ref, win_ref, c_ref, s1_ref, s2_ref, p_ref = refs
        x = x_ref[...]
    hx = _rms(x, n1g_ref[...]) * (1.0 + mod_ref[1:2, :]) + mod_ref[0:1, :]
    hb = hx.astype(BF16)

    def proj(lo, hi):
        return jnp.dot(hb, win_ref[:, lo:hi], preferred_element_type=F32)

    c, s1, s2 = c_ref[...], s1_ref[...], s2_ref[...]
    p_ref[:, P_Z:P_Z + W_B] = (proj(0, 256) * proj(256, 512)).astype(BF16)
    p_ref[:, P_K:P_K + LANES] = _rope(proj(512, 640), c, s1, s2).astype(BF16)
    p_ref[:, P_V:P_V + LANES] = proj(640, 768).astype(BF16)
    p_ref[:, P_U:P_U + W_A] = jax.nn.gelu(proj(768, 1024)).astype(BF16)
    p_ref[:, P_GV:P_GV + W_A] = _rms(jax.nn.gelu(proj(1024, 1280)), gg_ref[...]).astype(BF16)
    p_ref[:, P_BG:P_BG + W_B] = proj(1280, 1536).astype(BF16)
    for i in range(W_C // LANES):
        q = _rope(proj(1536 + i * LANES, 1536 + (i + 1) * LANES), c, s1, s2) * ATTN_SCALE
        p_ref[:, P_Q + i * LANES:P_Q + (i + 1) * LANES] = q.astype(BF16)


def _inproj(layer, x_all, y_all, mod4, n1g, gg, win, rope_tabs, n_lat, seq, batch):
    n_all = x_all.shape[0]
    tt = TT_PROJ
    tiles_per_batch = seq // tt
    n_lat_tiles = n_lat // tt
    has_y = y_all is not None

    def mod_row(i):
        return jnp.minimum(i // tiles_per_batch, batch)

    def rope_blk(i):
        return jnp.where(i < n_lat_tiles, i % tiles_per_batch, tiles_per_batch)

    row_spec = pl.BlockSpec((tt, D_MODEL), lambda i: (i, 0))
    in_specs = [row_spec]
    args = [x_all]
    if has_y:
        in_specs.append(row_spec)
        args.append(y_all)
    in_specs.append(pl.BlockSpec((None, None, 6, D_MODEL), lambda i: (layer, mod_row(i), 0, 0)))
    args.append(mod4)
    if has_y:
        in_specs.append(
            pl.BlockSpec((None, None, 6, D_MODEL), lambda i: (layer - 1, mod_row(i), 0, 0)))
        args.append(mod4)
    in_specs += [
        pl.BlockSpec((1, D_MODEL), lambda i: (0, 0)),
        pl.BlockSpec((1, W_A), lambda i: (0, 0)),
        pl.BlockSpec((D_MODEL, PROJ_COLS), lambda i: (0, 0)),
    ]
    args += [n1g, gg, win]
    for tab in rope_tabs:
        in_specs.append(pl.BlockSpec((tt, LANES), lambda i: (rope_blk(i), 0)))
        args.append(tab)
    p_shape = jax.ShapeDtypeStruct((n_all, P_COLS), BF16)
    p_spec = pl.BlockSpec((tt, P_COLS), lambda i: (i, 0))
    if has_y:
        out_shape = (jax.ShapeDtypeStruct((n_all, D_MODEL), F32), p_shape)
        out_specs = (row_spec, p_spec)
    else:
        out_shape, out_specs = p_shape, p_spec
    out = pl.pallas_call(
        functools.partial(_inproj_body, has_y),
        grid=(n_all // tt,),
        in_specs=in_specs, out_specs=out_specs, out_shape=out_shape,
        compiler_params=_cparams(1),
        name="inproj",
    )(*args)
    return out if has_y else (x_all, out)


def _route(logits_t, rb):
    sc = jax.nn.sigmoid(logits_t)
    sel = sc + rb
    sel_r = [sel[e:e + 1, :] for e in range(N_EXPERTS)]
    sc_r = [sc[e:e + 1, :] for e in range(N_EXPERTS)]
    best = None
    for g in range(N_GROUPS):
        v = sel_r[g * 4:g * 4 + 4]
        gs = None
        for (i, j) in PAIRS:
            pair = v[i] + v[j]
            gs = pair if gs is None else jnp.maximum(gs, pair)
        if best is None:
            best_score, best = gs, jnp.zeros_like(gs)
            vb = list(v)
            sb = list(sc_r[0:4])
        else:
            better = gs > best_score
            best_score = jnp.where(better, gs, best_score)
            best = jnp.where(better, float(g), best)
            vb = [jnp.where(better, v[i], vb[i]) for i in range(4)]
            sb = [jnp.where(better, sc_r[g * 4 + i], sb[i]) for i in range(4)]
    chosen = []
    for i in range(4):
        rank = jnp.zeros_like(best)
        for j in range(4):
            if j == i:
                continue
            beats = (vb[j] >= vb[i]) if j < i else (vb[j] > vb[i])
            rank = rank + jnp.where(beats, 1.0, 0.0)
        chosen.append(rank < 2.0)
    pair_id = jnp.zeros_like(best)
    w_lo = jnp.zeros_like(best)
    w_hi = jnp.zeros_like(best)
    for p, (i, j) in enumerate(PAIRS):
        hit = chosen[i] & chosen[j]
        tot = sb[i] + sb[j]
        pair_id = jnp.where(hit, float(p), pair_id)
        w_lo = jnp.where(hit, sb[i] / tot, w_lo)
        w_hi = jnp.where(hit, sb[j] / tot, w_hi)
    return best * float(N_PAIRS) + pair_id, w_lo, w_hi


def _gmlp_rows(p_ref, ws_ref, bs_ref, mix_ref, row0, n_rows):
    lane_grp = lax.broadcasted_iota(jnp.int32, (A_CHUNK, W_A), 1) // (W_A // A_GROUPS)
    for c in range(n_rows // A_CHUNK):
        rows = slice(row0 + c * A_CHUNK, row0 + (c + 1) * A_CHUNK)
        vch = p_ref[rows, P_GV:P_GV + W_A]
        s = bs_ref[...]
        for h in range(A_GROUPS):
            sh = jnp.dot(ws_ref[h], vch, preferred_element_type=F32)
            s = s + jnp.where(lane_grp == h, sh, 0.0)
        mix_ref[rows, 0:W_A] = (p_ref[rows, P_U:P_U + W_A].astype(F32) * s).astype(BF16)


def _conv_rows(p_ref, cw_ref, mix_ref, row0, n_rows, prev_row, next_row):
    rows = slice(row0, row0 + n_rows)
    z = p_ref[rows, P_Z:P_Z + W_B].astype(F32)
    ridx = lax.broadcasted_iota(jnp.int32, (n_rows, W_B), 0)
    z_up = jnp.where(ridx == 0, prev_row, pltpu.roll(z, 1, axis=0))
    z_dn = jnp.where(ridx == n_rows - 1, next_row, pltpu.roll(z, n_rows - 1, axis=0))
    conv = z_up * cw_ref[0:1, :] + z * cw_ref[1:2, :] + z_dn * cw_ref[2:3, :]
    mix_ref[rows, W_A:W_A + W_B] = (p_ref[rows, P_BG:P_BG + W_B].astype(F32) * conv).astype(BF16)


def _dup_heads(t):
    lane_lo = lax.broadcasted_iota(jnp.int32, (1, LANES), 1) < HEAD_DIM
    tf = t.astype(F32)
    tsw = pltpu.roll(tf, HEAD_DIM, axis=1)
    return (jnp.where(lane_lo, tf, tsw).astype(BF16), jnp.where(lane_lo, tsw, tf).astype(BF16))


def _attend_rows(p_ref, sink_ref, mix_ref, row0, n_qblocks, kc_d, vc_d, local):
    lane_lo = lax.broadcasted_iota(jnp.int32, (1, LANES), 1) < HEAD_DIM
    row_even = lax.broadcasted_iota(jnp.int32, (2 * BLOCK, 1), 0) < BLOCK
    zero_q = jnp.zeros((BLOCK, LANES), BF16)
    nt_dims = (((1,), (1,)), ((), ()))
    n_loc = 3 * BLOCK
    if local is not None:
        kl_d, vl_d, first_tile, last_tile = local
        col = lax.broadcasted_iota(jnp.int32, (2 * BLOCK, BLOCK), 1)
        qrow = lax.broadcasted_iota(jnp.int32, (2 * BLOCK, BLOCK), 0) % BLOCK
    for qb in range(n_qblocks):
        rows = slice(row0 + qb * BLOCK, row0 + (qb + 1) * BLOCK)
        if local is not None:
            lo_b = jnp.where(first_tile & (qb == 0), BLOCK, 0)
            hi_b = jnp.where(last_tile & (qb == n_qblocks - 1), 0, BLOCK)
            valid_prev = col >= jnp.maximum(qrow, lo_b)
            valid_next = col <= jnp.minimum(qrow, hi_b - 1)
        for i in range(W_C // LANES):
            kv = i // 2
            q = p_ref[rows, P_Q + i * LANES:P_Q + (i + 1) * LANES]
            qz = jnp.concatenate([jnp.where(lane_lo, q, zero_q), jnp.where(lane_lo, zero_q, q)], 0)
            sk = jnp.where(row_even, sink_ref[2 * i], sink_ref[2 * i + 1])
            s_ctx = lax.dot_general(qz, kc_d[kv], nt_dims, preferred_element_type=F32)
            m = jnp.maximum(jnp.max(s_ctx, axis=1, keepdims=True), sk)
            if local is not None:
                k_loc = kl_d[kv][qb * BLOCK:qb * BLOCK + n_loc]
                s_loc = lax.dot_general(qz, k_loc, nt_dims, preferred_element_type=F32)
                s_loc = jnp.concatenate(
                    [jnp.where(valid_prev, s_loc[:, 0:BLOCK], -jnp.inf), s_loc[:, BLOCK:2 * BLOCK],
                     jnp.where(valid_next, s_loc[:, 2 * BLOCK:n_loc], -jnp.inf)], axis=1)
                m = jnp.maximum(m, jnp.max(s_loc, axis=1, keepdims=True))
            p_ctx = jnp.exp(s_ctx - m)
            den = jnp.sum(p_ctx, axis=1, keepdims=True) + jnp.exp(sk - m)
            o = jnp.dot(p_ctx.astype(BF16), vc_d[kv], preferred_element_type=F32)
            if local is not None:
                p_loc = jnp.exp(s_loc - m)
                den = den + jnp.sum(p_loc, axis=1, keepdims=True)
                v_loc = vl_d[kv][qb * BLOCK:qb * BLOCK + n_loc]
                o = o + jnp.dot(p_loc.astype(BF16), v_loc, preferred_element_type=F32)
            o = o / den
            slab = jnp.where(lane_lo, o[0:BLOCK], o[BLOCK:2 * BLOCK])
            c0 = W_A + W_B + i * LANES
            mix_ref[rows, c0:c0 + LANES] = slab.astype(BF16)


def _mixer_body(tt, n_seq_tiles, n_lat_tiles, ctx_len, has_ctx,
                p_ref, hp_ref, hn_ref, pc_ref, x_ref, mod_ref, ws_ref, bs_ref, cw_ref, sink_ref,
                wout_ref, n2g_ref, rwt_ref, rb_ref, xo_ref, h2_ref, ro_ref, mix_ref):
    t = pl.program_id(0)

    def latent_tile():
        j = t % n_seq_tiles
        first_tile, last_tile = j == 0, j == n_seq_tiles - 1
        _gmlp_rows(p_ref, ws_ref, bs_ref, mix_ref, 0, tt)
        prev_row = hp_ref[BLOCK - 16:BLOCK, P_Z:P_Z + W_B].astype(F32)[15:16, :]
        next_row = hn_ref[0:16, P_Z:P_Z + W_B].astype(F32)[0:1, :]
        _conv_rows(p_ref, cw_ref, mix_ref, 0, tt,
                   jnp.where(first_tile, 0.0, prev_row), jnp.where(last_tile, 0.0, next_row))
        k_ext = jnp.concatenate(
            [hp_ref[:, P_K:P_K + LANES], p_ref[:, P_K:P_K + LANES], hn_ref[:, P_K:P_K + LANES]], 0)
        v_ext = jnp.concatenate(
            [hp_ref[:, P_V:P_V + LANES], p_ref[:, P_V:P_V + LANES], hn_ref[:, P_V:P_V + LANES]], 0)
        _attend_rows(p_ref, sink_ref, mix_ref, 0, tt // BLOCK,
                     _dup_heads(pc_ref[:, P_K:P_K + LANES]), _dup_heads(pc_ref[:, P_V:P_V + LANES]),
                     (_dup_heads(k_ext), _dup_heads(v_ext), first_tile, last_tile))

    def context_tile():
        zero_row = jnp.zeros((1, W_B), F32)
        for seg in range(tt // ctx_len):
            r0 = seg * ctx_len
            _gmlp_rows(p_ref, ws_ref, bs_ref, mix_ref, r0, ctx_len)
            _conv_rows(p_ref, cw_ref, mix_ref, r0, ctx_len, zero_row, zero_row)
            _attend_rows(p_ref, sink_ref, mix_ref, r0, ctx_len // BLOCK,
                         _dup_heads(p_ref[r0:r0 + ctx_len, P_K:P_K + LANES]),
                         _dup_heads(p_ref[r0:r0 + ctx_len, P_V:P_V + LANES]), None)

    if has_ctx:
        pl.when(t < n_lat_tiles)(latent_tile)
        pl.when(t >= n_lat_tiles)(context_tile)
    else:
        latent_tile()

    mixed = jnp.dot(mix_ref[...], wout_ref[...], preferred_element_type=F32)
    x = x_ref[...] + mod_ref[2:3, :] * mixed
    xo_ref[...] = x
    h2 = _rms(x, n2g_ref[...]) * (1.0 + mod_ref[4:5, :]) + mod_ref[3:4, :]
    h2_ref[:, 0:D_MODEL] = h2
    logits_t = lax.dot_general(rwt_ref[...], h2.astype(BF16), (((1,), (1,)), ((), ())),
                               preferred_element_type=F32)
    cls, w_lo, w_hi = _route(logits_t, rb_ref[...])
    rid = lax.broadcasted_iota(jnp.int32, (SUBLANES, tt), 0)
    ro = jnp.where(rid == 0, cls, jnp.where(rid == 1, w_lo, jnp.where(rid == 2, w_hi, 0.0)))
    ro_ref[...] = ro
    ro_pad = jnp.concatenate([ro, jnp.zeros((LANES - SUBLANES, tt), F32)], axis=0)
    for c in range(tt // LANES):
        h2_ref[c * LANES:(c + 1) * LANES, D_MODEL:H2_COLS] = ro_pad[:, c * LANES:(c + 1) * LANES].T


def _mixer(layer, has_ctx, p_all, x_all, mod4, wts, n_lat, seq, batch, ctx_len):
    ws, bs, cw, sink, wout, n2g, rwt, rb = wts
    tt = TT_MIX
    n_seq_tiles = seq // tt
    n_lat_tiles = n_lat // tt
    n_rows = p_all.shape[0] if has_ctx else n_lat
    n_tiles = n_rows // tt
    bpt = tt // BLOCK
    bps = seq // BLOCK

    def lat(t):
        tl = jnp.minimum(t, n_lat_tiles - 1)
        return tl, tl // n_seq_tiles

    def prev_map(t):
        tl, b = lat(t)
        return (jnp.maximum(tl * bpt - 1, b * bps), 0)

    def next_map(t):
        tl, b = lat(t)
        return (jnp.minimum((tl + 1) * bpt, (b + 1) * bps - 1), 0)

    row_map = lambda t: (t, 0)
    full = lambda *shape: pl.BlockSpec(shape, lambda t: (0,) * len(shape))
    in_specs = [
        pl.BlockSpec((tt, P_COLS), row_map),
        pl.BlockSpec((BLOCK, HALO_COLS), prev_map),
        pl.BlockSpec((BLOCK, HALO_COLS), next_map),
        pl.BlockSpec((ctx_len, HALO_COLS), lambda t: (n_lat // ctx_len + lat(t)[1], 0)),
        pl.BlockSpec((tt, D_MODEL), row_map),
        pl.BlockSpec((None, None, 6, D_MODEL),
                     lambda t: (layer, jnp.where(t < n_lat_tiles, lat(t)[1], batch), 0, 0)),
        full(A_GROUPS, A_CHUNK, A_CHUNK), full(A_CHUNK, W_A), full(8, W_B),
        pl.BlockSpec(memory_space=pltpu.SMEM),
        full(D_MODEL, D_MODEL), full(1, D_MODEL), full(N_EXPERTS, D_MODEL), full(N_EXPERTS, 1),
    ]
    out_shape = [jax.ShapeDtypeStruct((n_rows, D_MODEL), F32),
                 jax.ShapeDtypeStruct((n_rows, H2_COLS), F32),
                 jax.ShapeDtypeStruct((n_tiles, SUBLANES, tt), F32)]
    out_specs = [pl.BlockSpec((tt, D_MODEL), row_map), pl.BlockSpec((tt, H2_COLS), row_map),
                 pl.BlockSpec((None, SUBLANES, tt), lambda t: (t, 0, 0))]
    return pl.pallas_call(
        functools.partial(_mixer_body, tt, n_seq_tiles, n_lat_tiles, ctx_len, has_ctx),
        grid=(n_tiles,), in_specs=in_specs, out_specs=out_specs, out_shape=out_shape,
        scratch_shapes=[pltpu.VMEM((tt, D_MODEL), BF16)],
        compiler_params=_cparams(1),
        name="mixer",
    )(p_all, p_all, p_all, p_all, x_all, mod4, ws, bs, cw, sink, wout, n2g, rwt, rb)


def _moe_body(tm, n_tiles, e1_ref, e2_ref, nv_ref, tok_ref, tokn_ref, h_hbm,
              wg1, wu1, wd1, wg2, wu2, wd2, y_hbm, hbuf, ybuf, sem_g, sem_s):
    i = pl.program_id(0)
    nv = nv_ref[i]
    nv_next = jnp.where(i + 1 < n_tiles, nv_ref[jnp.minimum(i + 1, n_tiles - 1)], 0)

    def gather_row(idx_ref, dst_slot, r):
        t = idx_ref[0, 0, r]
        pltpu.make_async_copy(
            h_hbm.at[pl.ds(t, 1)], hbuf.at[dst_slot, pl.ds(r, 1)], sem_g.at[dst_slot]).start()

    def scatter_row(src_slot, r):
        t = tok_ref[0, 0, r]
        pltpu.make_async_copy(
            ybuf.at[src_slot, pl.ds(r, 1)], y_hbm.at[pl.ds(t, 1)], sem_s.at[src_slot]).start()

    def wait_scatter(src_slot, n_rows):
        n_grp = pl.multiple_of((n_rows // SUBLANES) * SUBLANES, SUBLANES)

        @pl.when(n_grp > 0)
        def _():
            pltpu.make_async_copy(ybuf.at[src_slot, pl.ds(0, n_grp)], y_hbm.at[pl.ds(0, n_grp)],
                                  sem_s.at[src_slot]).wait()

        def wait_row(r, carry):
            pltpu.make_async_copy(ybuf.at[src_slot, pl.ds(0, 1)], y_hbm.at[pl.ds(0, 1)],
                                  sem_s.at[src_slot]).wait()
            return carry

        lax.fori_loop(n_grp, n_rows, wait_row, 0)

    def tile(slot):
        @pl.when(nv_next > 0)
        def _():
            for r in range(tm):
                gather_row(tokn_ref, 1 - slot, r)

        pltpu.make_async_copy(h_hbm.at[pl.ds(0, tm)], hbuf.at[slot], sem_g.at[slot]).wait()

        @pl.when(i >= 2)
        def _():
            wait_scatter(slot, nv_ref[jnp.maximum(i - 2, 0)])

        rows = hbuf[slot]
        h = rows[:, 0:D_MODEL].astype(BF16)
        w_lo = rows[:, D_MODEL + 1:D_MODEL + 2]
        w_hi = rows[:, D_MODEL + 2:D_MODEL + 3]

        def expert(wg, wu, wd):
            g = jnp.dot(h, wg[...], preferred_element_type=F32)
            u = jnp.dot(h, wu[...], preferred_element_type=F32)
            a = (jax.nn.silu(g) * u).astype(BF16)
            return jnp.dot(a, wd[...], preferred_element_type=F32)

        ybuf[slot] = w_lo * expert(wg1, wu1, wd1) + w_hi * expert(wg2, wu2, wd2)

        @pl.when(nv == tm)
        def _():
            for r in range(tm):
                scatter_row(slot, r)

        @pl.when(nv < tm)
        def _():
            def body(r, carry):
                scatter_row(slot, r)
                return carry

            lax.fori_loop(0, nv, body, 0)

        @pl.when(nv_next == 0)
        def _():
            @pl.when(i >= 1)
            def _():
                wait_scatter(1 - slot, nv_ref[jnp.maximum(i - 1, 0)])

            wait_scatter(slot, nv)

    @pl.when((i == 0) & (nv > 0))
    def _():
        def body(r, carry):
            gather_row(tok_ref, 0, r)
            return carry

        lax.fori_loop(0, tm, body, 0, unroll=8)

    for slot in range(2):
        pl.when((nv > 0) & (i % 2 == slot))(functools.partial(tile, slot))


def _route_plan(cls, n_tok, tm):
    n_tiles = n_tok // tm + N_CLASSES
    order = jnp.argsort(cls, stable=True).astype(jnp.int32)
    counts = jnp.sum((cls[:, None] == jnp.arange(N_CLASSES)[None, :]).astype(jnp.int32), axis=0)
    tiles_per = (counts + tm - 1) // tm
    tile_end = jnp.cumsum(tiles_per)
    tile_start = tile_end - tiles_per
    tok_start = jnp.cumsum(counts) - counts
    ti = jnp.arange(n_tiles, dtype=jnp.int32)
    used = ti < tile_end[-1]
    last_used = jnp.maximum(tile_end[-1] - 1, 0)
    tcls = jnp.sum((jnp.minimum(ti, last_used)[:, None] >= tile_end[None, :]).astype(jnp.int32), axis=1)
    tcls = jnp.minimum(tcls, N_CLASSES - 1)
    off = (ti - tile_start[tcls]) * tm
    nv = jnp.where(used, jnp.clip(counts[tcls] - off, 0, tm), 0).astype(jnp.int32)
    r = jnp.arange(tm, dtype=jnp.int32)[None, :]
    ok = r < nv[:, None]
    pos = jnp.clip(tok_start[tcls][:, None] + off[:, None] + r, 0, n_tok - 1)
    tok = jnp.where(ok, order[pos], 0).astype(jnp.int32)
    grp = tcls // N_PAIRS
    pair = jnp.asarray(PAIRS, jnp.int32)[tcls % N_PAIRS]
    e1 = (grp * EXPERTS_PER_GROUP + pair[:, 0]).astype(jnp.int32)
    e2 = (grp * EXPERTS_PER_GROUP + pair[:, 1]).astype(jnp.int32)
    return e1, e2, nv, tok.reshape(n_tiles, 1, tm)


def _moe(h2_all, plan, wgate, wup, wdown, n_tok):
    tm = TM_MOE
    e1, e2, nv, tok = plan
    n_tiles = tok.shape[0]
    smem_blk = lambda fn: pl.BlockSpec((1, 1, tm), fn, memory_space=pltpu.SMEM)
    wsel = lambda shape, which: pl.BlockSpec(
        (None,) + shape, lambda i, e1r, e2r, nvr: ((e1r, e2r)[which][i], 0, 0))
    grid_spec = pltpu.PrefetchScalarGridSpec(
        num_scalar_prefetch=3,
        grid=(n_tiles,),
        in_specs=[
            smem_blk(lambda i, *_: (i, 0, 0)),
            smem_blk(lambda i, *_: (jnp.minimum(i + 1, n_tiles - 1), 0, 0)),
            pl.BlockSpec(memory_space=pl.ANY),
            wsel((D_MODEL, D_EXPERT), 0), wsel((D_MODEL, D_EXPERT), 0), wsel((D_EXPERT, D_MODEL), 0),
            wsel((D_MODEL, D_EXPERT), 1), wsel((D_MODEL, D_EXPERT), 1), wsel((D_EXPERT, D_MODEL), 1),
        ],
        out_specs=pl.BlockSpec(memory_space=pl.ANY),
        scratch_shapes=[
            pltpu.VMEM((2, tm, H2_COLS), F32), pltpu.VMEM((2, tm, D_MODEL), F32),
            pltpu.SemaphoreType.DMA((2,)), pltpu.SemaphoreType.DMA((2,)),
        ],
    )
    return pl.pallas_call(
        functools.partial(_moe_body, tm, n_tiles),
        grid_spec=grid_spec,
        out_shape=jax.ShapeDtypeStruct((n_tok, D_MODEL), F32),
        compiler_params=_cparams(1),
        name="moe",
    )(e1, e2, nv, tok, tok, h2_all, wgate, wup, wdown, wgate, wup, wdown)


def _final_body(x_ref, y_ref, mod_ref, g_ref, o_ref):
    x = x_ref[...] + mod_ref[5:6, :] * y_ref[...]
    o_ref[...] = _rms(x, g_ref[...])


def _final(layer, x_all, y_all, mod4, final_g, n_lat, seq):
    tt = TT_PROJ
    tiles_per_batch = seq // tt
    row_spec = pl.BlockSpec((tt, D_MODEL), lambda i: (i, 0))
    return pl.pallas_call(
        _final_body,
        grid=(n_lat // tt,),
        in_specs=[row_spec, row_spec,
                  pl.BlockSpec((None, None, 6, D_MODEL), lambda i: (layer, i // tiles_per_batch, 0, 0)),
                  pl.BlockSpec((1, D_MODEL), lambda i: (0, 0))],
        out_specs=row_spec,
        out_shape=jax.ShapeDtypeStruct((n_lat, D_MODEL), F32),
        compiler_params=_cparams(1),
        name="final_norm",
    )(x_all, y_all, mod4, final_g)


def _rope_tables(seq, pad_rows):
    rows = seq // GRID_W
    row = jnp.repeat(jnp.arange(rows, dtype=F32), GRID_W)
    col = jnp.tile(jnp.arange(GRID_W, dtype=F32), rows)
    inv = ROPE_BASE ** (-jnp.arange(QUARTER, dtype=F32) / QUARTER)
    ang = jnp.stack([row[:, None] * inv, col[:, None] * inv], axis=1)
    cos, sin = jnp.cos(ang), jnp.sin(ang)
    zero = jnp.zeros_like(sin)
    c64 = jnp.stack([cos, cos], axis=2).reshape(seq, HEAD_DIM)
    s1 = jnp.stack([-sin, zero], axis=2).reshape(seq, HEAD_DIM)
    s2 = jnp.stack([zero, sin], axis=2).reshape(seq, HEAD_DIM)
    out = []
    for t, fill in ((c64, 1.0), (s1, 0.0), (s2, 0.0)):
        t = jnp.concatenate([t, t], axis=1)
        out.append(jnp.concatenate([t, jnp.full((pad_rows, LANES), fill, F32)], axis=0))
    return out


def _permute_w_in(w_in):
    return jnp.concatenate(
        [w_in[..., 768:1024], w_in[..., 1024:1280], w_in[..., 1792:1920], w_in[..., 1920:2048],
         w_in[..., 0:256], w_in[..., 256:512], w_in[..., 512:768], w_in[..., 1280:1792]], axis=-1)


def kernel(x, c, ctx, c_ctx, w_mod, b_mod, norm1_g, norm2_g, w_in, gmlp_g, w_s, b_s, conv_w,
           attn_sink, w_out, router_w, router_b, w_gate, w_up, w_down, final_g):
    batch, seq, d = x.shape
    ctx_len = ctx.shape[1]
    depth = w_mod.shape[0]
    n_lat = batch * seq
    n_ctx = batch * ctx_len
    assert d == D_MODEL and seq % TT_MIX == 0 and n_ctx % TT_PROJ == 0 and n_ctx % TT_MIX == 0
    assert TT_MIX % ctx_len == 0 and ctx_len % BLOCK == 0 and n_lat % ctx_len == 0
    assert batch < MOD_ROWS

    c_all = jnp.concatenate(
        [c, c_ctx[None], jnp.zeros((MOD_ROWS - batch - 1, d), F32)], axis=0)
    mod4 = _modulation(c_all, w_mod, b_mod).reshape(depth, MOD_ROWS, 6, d)

    rope_tabs = _rope_tables(seq, TT_PROJ)
    win_b = _permute_w_in(w_in).astype(BF16)
    wout_b = w_out.astype(BF16)
    ws_b = w_s.astype(BF16)
    wg_b, wu_b, wd_b = w_gate.astype(BF16), w_up.astype(BF16), w_down.astype(BF16)
    rwt = router_w.T.astype(BF16)
    rb = router_b.reshape(N_EXPERTS, 1).astype(F32)
    bs_full = jnp.repeat(jnp.swapaxes(b_s, 1, 2), W_A // A_GROUPS, axis=2)
    cw_pad = jnp.concatenate([conv_w, jnp.zeros((depth, 5, W_B), F32)], axis=1)

    x_all = jnp.concatenate([x.reshape(n_lat, d), ctx.reshape(n_ctx, d)], axis=0)
    y_all = None
    for l in range(depth):
        last = l == depth - 1
        x_all, p_all = _inproj(l, x_all, y_all, mod4, norm1_g[l][None], gmlp_g[l][None], win_b[l],
                               rope_tabs, n_lat, seq, batch)
        wts = (ws_b[l], bs_full[l], cw_pad[l], attn_sink[l], wout_b[l], norm2_g[l][None], rwt, rb)
        x_all, h2, ro = _mixer(l, not last, p_all, x_all, mod4, wts, n_lat, seq, batch, ctx_len)
        n_tok = h2.shape[0]
        cls = ro[:, 0, :].reshape(-1).astype(jnp.int32)
        plan = _route_plan(cls, n_tok, TM_MOE)
        y_all = _moe(h2, plan, wg_b[l], wu_b[l], wd_b[l], n_tok)
    out = _final(depth - 1, x_all, y_all, mod4, final_g[None], n_lat, seq)
    return out.reshape(batch, seq, d)
```

```python
import functools

import jax
import jax.numpy as jnp
from jax import lax
from jax.experimental import pallas as pl
from jax.experimental.pallas import tpu as pltpu

F32 = jnp.float32
BF16 = jnp.bfloat16

D_MODEL = 1024
W_A = 256
A_GROUPS = 4
A_CHUNK = 128
W_B = 256
HEAD_DIM = 64
W_C = 512
N_HEADS = 8
N_KV = 2
BLOCK = 128
GRID_W = 64
QUARTER = 16
ROPE_BASE = 10000.0
ATTN_SCALE = HEAD_DIM ** -0.5
EPS = 1e-6
N_EXPERTS = 16
N_GROUPS = 4
EXPERTS_PER_GROUP = 4
D_EXPERT = 512
N_PAIRS = 6
N_CLASSES = N_GROUPS * N_PAIRS
PAIRS = ((0, 1), (0, 2), (0, 3), (1, 2), (1, 3), (2, 3))
TOKEN_BITS = 17

LANES = 128
SUBLANES = 8
MOD_ROWS = 16
PROJ_COLS = 2048
P_Z, P_K, P_V, P_U, P_GV, P_BG, P_Q = 0, 256, 384, 512, 768, 1024, 1280
P_COLS = 1792
HALO_COLS = 512
H2_COLS = D_MODEL + LANES

TT_PROJ = 512
TT_MIX = 512
TM_MOE = 256
VMEM_LIMIT = 52 * 1024 * 1024


def _cparams(n_axes):
    return pltpu.CompilerParams(
        dimension_semantics=("arbitrary",) * n_axes, vmem_limit_bytes=VMEM_LIMIT)


def _mod_body(c_ref, w_ref, b_ref, o_ref):
    a = jax.nn.silu(c_ref[...]).astype(BF16)
    o_ref[...] = jnp.dot(a, w_ref[...].astype(BF16), preferred_element_type=F32) + b_ref[...]


def _modulation(c_all, w_mod, b_mod):
    depth = w_mod.shape[0]
    tn = 1536
    return pl.pallas_call(
        _mod_body,
        grid=(depth, 6 * D_MODEL // tn),
        in_specs=[
            pl.BlockSpec((MOD_ROWS, D_MODEL), lambda l, n: (0, 0)),
            pl.BlockSpec((None, D_MODEL, tn), lambda l, n: (l, 0, n)),
            pl.BlockSpec((None, 1, tn), lambda l, n: (l, 0, n)),
        ],
        out_specs=pl.BlockSpec((None, MOD_ROWS, tn), lambda l, n: (l, 0, n)),
        out_shape=jax.ShapeDtypeStruct((depth, MOD_ROWS, 6 * D_MODEL), F32),
        compiler_params=_cparams(2),
        name="modulation",
    )(c_all, w_mod, b_mod.reshape(depth, 1, 6 * D_MODEL))


def _rms(x, g):
    return x * lax.rsqrt(jnp.mean(x * x, axis=-1, keepdims=True) + EPS) * g


def _rope(t, c, s1, s2):
    return t * c + pltpu.roll(t, LANES - QUARTER, axis=1) * s1 + pltpu.roll(t, QUARTER, axis=1) * s2


def _inproj_body(first, n_lat_tiles, *refs):
    if first:
        (xl_ref, xc_ref, mod_ref, n1g_ref, gg_ref, win_ref, c_ref, s1_ref, s2_ref,
         xo_ref, p_ref) = refs
        x = jnp.where(pl.program_id(0) < n_lat_tiles, xl_ref[...], xc_ref[...])
    else:
        (x_ref, y_ref, mod_ref, modp_ref, n1g_ref, gg_ref, win_ref, c_ref, s1_ref, s2_ref,
         xo_ref, p_ref) = refs
        x = x_ref[...] + modp_ref[5:6, :] * y_ref[...]
    xo_ref[...] = x
    hx = _rms(x, n1g_ref[...]) * (1.0 + mod_ref[1:2, :]) + mod_ref[0:1, :]
    hb = hx.astype(BF16)

    def proj(lo, hi):
        return jnp.dot(hb, win_ref[:, lo:hi], preferred_element_type=F32)

    c, s1, s2 = c_ref[...], s1_ref[...], s2_ref[...]
    p_ref[:, P_Z:P_Z + W_B] = (proj(0, 256) * proj(256, 512)).astype(BF16)
    p_ref[:, P_K:P_K + LANES] = _rope(proj(512, 640), c, s1, s2).astype(BF16)
    p_ref[:, P_V:P_V + LANES] = proj(640, 768).astype(BF16)
    p_ref[:, P_U:P_U + W_A] = jax.nn.gelu(proj(768, 1024)).astype(BF16)
    p_ref[:, P_GV:P_GV + W_A] = _rms(jax.nn.gelu(proj(1024, 1280)), gg_ref[...]).astype(BF16)
    p_ref[:, P_BG:P_BG + W_B] = proj(1280, 1536).astype(BF16)
    for i in range(W_C // LANES):
        q = _rope(proj(1536 + i * LANES, 1536 + (i + 1) * LANES), c, s1, s2) * ATTN_SCALE
        p_ref[:, P_Q + i * LANES:P_Q + (i + 1) * LANES] = q.astype(BF16)


def _inproj(layer, x_src, y_all, mod4, n1g, gg, win, rope_tabs, n_lat, n_all, seq, batch):
    tt = TT_PROJ
    tiles_per_batch = seq // tt
    n_lat_tiles = n_lat // tt
    first = y_all is None

    def mod_row(i):
        return jnp.minimum(i // tiles_per_batch, batch)

    def rope_blk(i):
        return jnp.where(i < n_lat_tiles, i % tiles_per_batch, tiles_per_batch)

    row_spec = pl.BlockSpec((tt, D_MODEL), lambda i: (i, 0))
    if first:
        in_specs = [
            pl.BlockSpec((tt, D_MODEL), lambda i: (jnp.minimum(i, n_lat_tiles - 1), 0)),
            pl.BlockSpec((tt, D_MODEL), lambda i: (jnp.maximum(i - n_lat_tiles, 0), 0)),
        ]
        args = list(x_src)
    else:
        in_specs = [row_spec, row_spec]
        args = [x_src, y_all]
    in_specs.append(pl.BlockSpec((None, None, 6, D_MODEL), lambda i: (layer, mod_row(i), 0, 0)))
    args.append(mod4)
    if not first:
        in_specs.append(
            pl.BlockSpec((None, None, 6, D_MODEL), lambda i: (layer - 1, mod_row(i), 0, 0)))
        args.append(mod4)
    in_specs += [
        pl.BlockSpec((1, D_MODEL), lambda i: (0, 0)),
        pl.BlockSpec((1, W_A), lambda i: (0, 0)),
        pl.BlockSpec((D_MODEL, PROJ_COLS), lambda i: (0, 0)),
    ]
    args += [n1g, gg, win]
    for tab in rope_tabs:
        in_specs.append(pl.BlockSpec((tt, LANES), lambda i: (rope_blk(i), 0)))
        args.append(tab)
    return pl.pallas_call(
        functools.partial(_inproj_body, first, n_lat_tiles),
        grid=(n_all // tt,),
        in_specs=in_specs,
        out_specs=(row_spec, pl.BlockSpec((tt, P_COLS), lambda i: (i, 0))),
        out_shape=(jax.ShapeDtypeStruct((n_all, D_MODEL), F32),
                   jax.ShapeDtypeStruct((n_all, P_COLS), BF16)),
        compiler_params=_cparams(1),
        name="inproj",
    )(*args)


def _route(logits_t, rb):
    sc = jax.nn.sigmoid(logits_t)
    sel = sc + rb
    sel_r = [sel[e:e + 1, :] for e in range(N_EXPERTS)]
    sc_r = [sc[e:e + 1, :] for e in range(N_EXPERTS)]
    best = None
    for g in range(N_GROUPS):
        v = sel_r[g * 4:g * 4 + 4]
        gs = None
        for (i, j) in PAIRS:
            pair = v[i] + v[j]
            gs = pair if gs is None else jnp.maximum(gs, pair)
        if best is None:
            best_score, best = gs, jnp.zeros_like(gs)
            vb = list(v)
            sb = list(sc_r[0:4])
        else:
            better = gs > best_score
            best_score = jnp.where(better, gs, best_score)
            best = jnp.where(better, float(g), best)
            vb = [jnp.where(better, v[i], vb[i]) for i in range(4)]
            sb = [jnp.where(better, sc_r[g * 4 + i], sb[i]) for i in range(4)]
    chosen = []
    for i in range(4):
        rank = jnp.zeros_like(best)
        for j in range(4):
            if j == i:
                continue
            beats = (vb[j] >= vb[i]) if j < i else (vb[j] > vb[i])
            rank = rank + jnp.where(beats, 1.0, 0.0)
        chosen.append(rank < 2.0)
    pair_id = jnp.zeros_like(best)
    w_lo = jnp.zeros_like(best)
    w_hi = jnp.zeros_like(best)
    for p, (i, j) in enumerate(PAIRS):
        hit = chosen[i] & chosen[j]
        tot = sb[i] + sb[j]
        pair_id = jnp.where(hit, float(p), pair_id)
        w_lo = jnp.where(hit, sb[i] / tot, w_lo)
        w_hi = jnp.where(hit, sb[j] / tot, w_hi)
    return best * float(N_PAIRS) + pair_id, w_lo, w_hi


def _gmlp_rows(p_ref, ws_ref, bs_ref, mix_ref, row0, n_rows):
    lane_grp = lax.broadcasted_iota(jnp.int32, (A_CHUNK, W_A), 1) // (W_A // A_GROUPS)
    for c in range(n_rows // A_CHUNK):
        rows = slice(row0 + c * A_CHUNK, row0 + (c + 1) * A_CHUNK)
        vch = p_ref[rows, P_GV:P_GV + W_A]
        s = bs_ref[...]
        for h in range(A_GROUPS):
            sh = jnp.dot(ws_ref[h], vch, preferred_element_type=F32)
            s = s + jnp.where(lane_grp == h, sh, 0.0)
        mix_ref[rows, 0:W_A] = (p_ref[rows, P_U:P_U + W_A].astype(F32) * s).astype(BF16)


def _conv_rows(p_ref, cw_ref, mix_ref, row0, n_rows, prev_row, next_row):
    rows = slice(row0, row0 + n_rows)
    z = p_ref[rows, P_Z:P_Z + W_B].astype(F32)
    ridx = lax.broadcasted_iota(jnp.int32, (n_rows, W_B), 0)
    z_up = jnp.where(ridx == 0, prev_row, pltpu.roll(z, 1, axis=0))
    z_dn = jnp.where(ridx == n_rows - 1, next_row, pltpu.roll(z, n_rows - 1, axis=0))
    conv = z_up * cw_ref[0:1, :] + z * cw_ref[1:2, :] + z_dn * cw_ref[2:3, :]
    mix_ref[rows, W_A:W_A + W_B] = (p_ref[rows, P_BG:P_BG + W_B].astype(F32) * conv).astype(BF16)


def _dup_heads(t):
    lane_lo = lax.broadcasted_iota(jnp.int32, (1, LANES), 1) < HEAD_DIM
    tf = t.astype(F32)
    tsw = pltpu.roll(tf, HEAD_DIM, axis=1)
    return (jnp.where(lane_lo, tf, tsw).astype(BF16), jnp.where(lane_lo, tsw, tf).astype(BF16))


def _attend_rows(p_ref, sink_ref, mix_ref, row0, n_qblocks, kc_d, vc_d, local):
    lane_lo = lax.broadcasted_iota(jnp.int32, (1, LANES), 1) < HEAD_DIM
    row_even = lax.broadcasted_iota(jnp.int32, (2 * BLOCK, 1), 0) < BLOCK
    zero_q = jnp.zeros((BLOCK, LANES), BF16)
    nt_dims = (((1,), (1,)), ((), ()))
    n_loc = 3 * BLOCK
    if local is not None:
        kl_d, vl_d, first_tile, last_tile = local
        col = lax.broadcasted_iota(jnp.int32, (2 * BLOCK, BLOCK), 1)
        qrow = lax.broadcasted_iota(jnp.int32, (2 * BLOCK, BLOCK), 0) % BLOCK
    for qb in range(n_qblocks):
        rows = slice(row0 + qb * BLOCK, row0 + (qb + 1) * BLOCK)
        if local is not None:
            lo_b = jnp.where(first_tile & (qb == 0), BLOCK, 0)
            hi_b = jnp.where(last_tile & (qb == n_qblocks - 1), 0, BLOCK)
            valid_prev = col >= jnp.maximum(qrow, lo_b)
            valid_next = col <= jnp.minimum(qrow, hi_b - 1)
        for i in range(W_C // LANES):
            kv = i // 2
            q = p_ref[rows, P_Q + i * LANES:P_Q + (i + 1) * LANES]
            qz = jnp.concatenate([jnp.where(lane_lo, q, zero_q), jnp.where(lane_lo, zero_q, q)], 0)
            sk = jnp.where(row_even, sink_ref[2 * i], sink_ref[2 * i + 1])
            s_ctx = lax.dot_general(qz, kc_d[kv], nt_dims, preferred_element_type=F32)
            m = jnp.maximum(jnp.max(s_ctx, axis=1, keepdims=True), sk)
            if local is not None:
                k_loc = kl_d[kv][qb * BLOCK:qb * BLOCK + n_loc]
                s_loc = lax.dot_general(qz, k_loc, nt_dims, preferred_element_type=F32)
                s_loc = jnp.concatenate(
                    [jnp.where(valid_prev, s_loc[:, 0:BLOCK], -jnp.inf), s_loc[:, BLOCK:2 * BLOCK],
                     jnp.where(valid_next, s_loc[:, 2 * BLOCK:n_loc], -jnp.inf)], axis=1)
                m = jnp.maximum(m, jnp.max(s_loc, axis=1, keepdims=True))
            p_ctx = jnp.exp(s_ctx - m)
            den = jnp.sum(p_ctx, axis=1, keepdims=True) + jnp.exp(sk - m)
            o = jnp.dot(p_ctx.astype(BF16), vc_d[kv], preferred_element_type=F32)
            if local is not None:
                p_loc = jnp.exp(s_loc - m)
                den = den + jnp.sum(p_loc, axis=1, keepdims=True)
                v_loc = vl_d[kv][qb * BLOCK:qb * BLOCK + n_loc]
                o = o + jnp.dot(p_loc.astype(BF16), v_loc, preferred_element_type=F32)
            o = o / den
            slab = jnp.where(lane_lo, o[0:BLOCK], o[BLOCK:2 * BLOCK])
            c0 = W_A + W_B + i * LANES
            mix_ref[rows, c0:c0 + LANES] = slab.astype(BF16)


def _mixer_body(tt, n_seq_tiles, n_lat_tiles, ctx_len, has_ctx,
                p_ref, hp_ref, hn_ref, pc_ref, x_ref, mod_ref, ws_ref, bs_ref, cw_ref, sink_ref,
                wout_ref, n2g_ref, rwt_ref, rb_ref, xo_ref, h2_ref, ro_ref, mix_ref):
    t = pl.program_id(0)

    def latent_tile():
        j = t % n_seq_tiles
        first_tile, last_tile = j == 0, j == n_seq_tiles - 1
        _gmlp_rows(p_ref, ws_ref, bs_ref, mix_ref, 0, tt)
        prev_row = hp_ref[BLOCK - 16:BLOCK, P_Z:P_Z + W_B].astype(F32)[15:16, :]
        next_row = hn_ref[0:16, P_Z:P_Z + W_B].astype(F32)[0:1, :]
        _conv_rows(p_ref, cw_ref, mix_ref, 0, tt,
                   jnp.where(first_tile, 0.0, prev_row), jnp.where(last_tile, 0.0, next_row))
        k_ext = jnp.concatenate(
            [hp_ref[:, P_K:P_K + LANES], p_ref[:, P_K:P_K + LANES], hn_ref[:, P_K:P_K + LANES]], 0)
        v_ext = jnp.concatenate(
            [hp_ref[:, P_V:P_V + LANES], p_ref[:, P_V:P_V + LANES], hn_ref[:, P_V:P_V + LANES]], 0)
        _attend_rows(p_ref, sink_ref, mix_ref, 0, tt // BLOCK,
                     _dup_heads(pc_ref[:, P_K:P_K + LANES]), _dup_heads(pc_ref[:, P_V:P_V + LANES]),
                     (_dup_heads(k_ext), _dup_heads(v_ext), first_tile, last_tile))

    def context_tile():
        zero_row = jnp.zeros((1, W_B), F32)
        for seg in range(tt // ctx_len):
            r0 = seg * ctx_len
            _gmlp_rows(p_ref, ws_ref, bs_ref, mix_ref, r0, ctx_len)
            _conv_rows(p_ref, cw_ref, mix_ref, r0, ctx_len, zero_row, zero_row)
            _attend_rows(p_ref, sink_ref, mix_ref, r0, ctx_len // BLOCK,
                         _dup_heads(p_ref[r0:r0 + ctx_len, P_K:P_K + LANES]),
                         _dup_heads(p_ref[r0:r0 + ctx_len, P_V:P_V + LANES]), None)

    if has_ctx:
        pl.when(t < n_lat_tiles)(latent_tile)
        pl.when(t >= n_lat_tiles)(context_tile)
    else:
        latent_tile()

    mixed = jnp.dot(mix_ref[...], wout_ref[...], preferred_element_type=F32)
    x = x_ref[...] + mod_ref[2:3, :] * mixed
    xo_ref[...] = x
    h2 = _rms(x, n2g_ref[...]) * (1.0 + mod_ref[4:5, :]) + mod_ref[3:4, :]
    h2_ref[:, 0:D_MODEL] = h2
    logits_t = lax.dot_general(rwt_ref[...], h2.astype(BF16), (((1,), (1,)), ((), ())),
                               preferred_element_type=F32)
    cls, w_lo, w_hi = _route(logits_t, rb_ref[...])
    rid = lax.broadcasted_iota(jnp.int32, (SUBLANES, tt), 0)
    ro = jnp.where(rid == 0, cls, jnp.where(rid == 1, w_lo, jnp.where(rid == 2, w_hi, 0.0)))
    ro_ref[...] = ro
    ro_pad = jnp.concatenate([ro, jnp.zeros((LANES - SUBLANES, tt), F32)], axis=0)
    for c in range(tt // LANES):
        h2_ref[c * LANES:(c + 1) * LANES, D_MODEL:H2_COLS] = ro_pad[:, c * LANES:(c + 1) * LANES].T


def _mixer(layer, has_ctx, p_all, x_all, mod4, wts, n_lat, seq, batch, ctx_len):
    ws, bs, cw, sink, wout, n2g, rwt, rb = wts
    tt = TT_MIX
    n_seq_tiles = seq // tt
    n_lat_tiles = n_lat // tt
    n_rows = p_all.shape[0] if has_ctx else n_lat
    n_tiles = n_rows // tt
    bpt = tt // BLOCK
    bps = seq // BLOCK

    def lat(t):
        tl = jnp.minimum(t, n_lat_tiles - 1)
        return tl, tl // n_seq_tiles

    def prev_map(t):
        tl, b = lat(t)
        return (jnp.maximum(tl * bpt - 1, b * bps), 0)

    def next_map(t):
        tl, b = lat(t)
        return (jnp.minimum((tl + 1) * bpt, (b + 1) * bps - 1), 0)

    row_map = lambda t: (t, 0)
    full = lambda *shape: pl.BlockSpec(shape, lambda t: (0,) * len(shape))
    in_specs = [
        pl.BlockSpec((tt, P_COLS), row_map),
        pl.BlockSpec((BLOCK, HALO_COLS), prev_map),
        pl.BlockSpec((BLOCK, HALO_COLS), next_map),
        pl.BlockSpec((ctx_len, HALO_COLS), lambda t: (n_lat // ctx_len + lat(t)[1], 0)),
        pl.BlockSpec((tt, D_MODEL), row_map),
        pl.BlockSpec((None, None, 6, D_MODEL),
                     lambda t: (layer, jnp.where(t < n_lat_tiles, lat(t)[1], batch), 0, 0)),
        full(A_GROUPS, A_CHUNK, A_CHUNK), full(A_CHUNK, W_A), full(8, W_B),
        pl.BlockSpec(memory_space=pltpu.SMEM),
        full(D_MODEL, D_MODEL), full(1, D_MODEL), full(N_EXPERTS, D_MODEL), full(N_EXPERTS, 1),
    ]
    out_shape = [jax.ShapeDtypeStruct((n_rows, D_MODEL), F32),
                 jax.ShapeDtypeStruct((n_rows, H2_COLS), F32),
                 jax.ShapeDtypeStruct((n_tiles, SUBLANES, tt), F32)]
    out_specs = [pl.BlockSpec((tt, D_MODEL), row_map), pl.BlockSpec((tt, H2_COLS), row_map),
                 pl.BlockSpec((None, SUBLANES, tt), lambda t: (t, 0, 0))]
    return pl.pallas_call(
        functools.partial(_mixer_body, tt, n_seq_tiles, n_lat_tiles, ctx_len, has_ctx),
        grid=(n_tiles,), in_specs=in_specs, out_specs=out_specs, out_shape=out_shape,
        scratch_shapes=[pltpu.VMEM((tt, D_MODEL), BF16)],
        compiler_params=_cparams(1),
        name="mixer",
    )(p_all, p_all, p_all, p_all, x_all, mod4, ws, bs, cw, sink, wout, n2g, rwt, rb)


def _moe_body(tm, n_tiles, e1_ref, e2_ref, nv_ref, tok_ref, tokn_ref, h_hbm,
              wg1, wu1, wd1, wg2, wu2, wd2, y_hbm, hbuf0, hbuf1, ybuf0, ybuf1, sem_g, sem_s):
    i = pl.program_id(0)
    nv = nv_ref[i]
    nv_next = jnp.where(i + 1 < n_tiles, nv_ref[jnp.minimum(i + 1, n_tiles - 1)], 0)
    hbuf, ybuf = (hbuf0, hbuf1), (ybuf0, ybuf1)

    def gather_row(idx_ref, dst_slot, r):
        t = idx_ref[0, 0, r]
        pltpu.make_async_copy(
            h_hbm.at[pl.ds(t, 1)], hbuf[dst_slot].at[pl.ds(r, 1)], sem_g.at[dst_slot]).start()

    def scatter_row(src_slot, r):
        t = tok_ref[0, 0, r]
        pltpu.make_async_copy(
            ybuf[src_slot].at[pl.ds(r, 1)], y_hbm.at[pl.ds(t, 1)], sem_s.at[src_slot]).start()

    def wait_gather(dst_slot):
        pltpu.make_async_copy(h_hbm.at[pl.ds(0, tm)], hbuf[dst_slot], sem_g.at[dst_slot]).wait()

    def wait_scatter(src_slot, n_rows):
        n_grp = pl.multiple_of((n_rows // SUBLANES) * SUBLANES, SUBLANES)

        @pl.when(n_grp > 0)
        def _():
            pltpu.make_async_copy(ybuf[src_slot].at[pl.ds(0, n_grp)], y_hbm.at[pl.ds(0, n_grp)],
                                  sem_s.at[src_slot]).wait()

        def wait_row(r, carry):
            pltpu.make_async_copy(ybuf[src_slot].at[pl.ds(0, 1)], y_hbm.at[pl.ds(0, 1)],
                                  sem_s.at[src_slot]).wait()
            return carry

        lax.fori_loop(n_grp, n_rows, wait_row, 0)

    def tile(slot):
        wait_gather(slot)

        @pl.when(i >= 2)
        def _():
            wait_scatter(slot, nv_ref[jnp.maximum(i - 2, 0)])

        for r in range(tm):
            gather_row(tokn_ref, 1 - slot, r)

        rows = hbuf[slot][...]
        h = rows[:, 0:D_MODEL].astype(BF16)
        w_lo = rows[:, D_MODEL + 1:D_MODEL + 2]
        w_hi = rows[:, D_MODEL + 2:D_MODEL + 3]

        def expert(wg, wu, wd):
            g = jnp.dot(h, wg[...], preferred_element_type=F32)
            u = jnp.dot(h, wu[...], preferred_element_type=F32)
            a = (jax.nn.silu(g) * u).astype(BF16)
            return jnp.dot(a, wd[...], preferred_element_type=F32)

        ybuf[slot][...] = w_lo * expert(wg1, wu1, wd1) + w_hi * expert(wg2, wu2, wd2)

        @pl.when(nv == tm)
        def _():
            for r in range(tm):
                scatter_row(slot, r)

        @pl.when(nv < tm)
        def _():
            def body(r, carry):
                scatter_row(slot, r)
                return carry

            lax.fori_loop(0, nv, body, 0)

        @pl.when(nv_next == 0)
        def _():
            @pl.when(i >= 1)
            def _():
                wait_scatter(1 - slot, nv_ref[jnp.maximum(i - 1, 0)])

            wait_scatter(slot, nv)

    @pl.when((i == 0) & (nv > 0))
    def _():
        def body(r, carry):
            gather_row(tok_ref, 0, r)
            return carry

        lax.fori_loop(0, tm, body, 0, unroll=8)

    first_unused = (nv == 0) & (i > 0) & (nv_ref[jnp.maximum(i - 1, 0)] > 0)
    for slot in range(2):
        pl.when((nv > 0) & (i % 2 == slot))(functools.partial(tile, slot))
        pl.when(first_unused & (i % 2 == slot))(functools.partial(wait_gather, slot))


def _route_plan(cls, n_tok, tm):
    assert n_tok % tm == 0 and n_tok <= (1 << TOKEN_BITS)
    n_tiles = n_tok // tm + N_CLASSES
    packed = jnp.sort(cls * (1 << TOKEN_BITS) + jnp.arange(n_tok, dtype=jnp.int32))
    order = packed & ((1 << TOKEN_BITS) - 1)
    counts = jnp.sum((cls[:, None] == jnp.arange(N_CLASSES)[None, :]).astype(jnp.int32), axis=0)
    tiles_per = (counts + tm - 1) // tm
    tile_end = jnp.cumsum(tiles_per)
    tile_start = tile_end - tiles_per
    tok_start = jnp.cumsum(counts) - counts
    ti = jnp.arange(n_tiles, dtype=jnp.int32)
    used = ti < tile_end[-1]
    last_used = jnp.maximum(tile_end[-1] - 1, 0)
    tcls = jnp.sum((jnp.minimum(ti, last_used)[:, None] >= tile_end[None, :]).astype(jnp.int32), axis=1)
    tcls = jnp.minimum(tcls, N_CLASSES - 1)
    off = (ti - tile_start[tcls]) * tm
    nv = jnp.where(used, jnp.clip(counts[tcls] - off, 0, tm), 0).astype(jnp.int32)
    r = jnp.arange(tm, dtype=jnp.int32)[None, :]
    ok = r < nv[:, None]
    pos = jnp.clip(tok_start[tcls][:, None] + off[:, None] + r, 0, n_tok - 1)
    tok = jnp.where(ok, order[pos], 0).astype(jnp.int32)
    grp = tcls // N_PAIRS
    pair = jnp.asarray(PAIRS, jnp.int32)[tcls % N_PAIRS]
    e1 = (grp * EXPERTS_PER_GROUP + pair[:, 0]).astype(jnp.int32)
    e2 = (grp * EXPERTS_PER_GROUP + pair[:, 1]).astype(jnp.int32)
    return e1, e2, nv, tok.reshape(n_tiles, 1, tm)


def _moe(h2_all, plan, wgate, wup, wdown, n_tok):
    tm = TM_MOE
    e1, e2, nv, tok = plan
    n_tiles = tok.shape[0]
    smem_blk = lambda fn: pl.BlockSpec((1, 1, tm), fn, memory_space=pltpu.SMEM)
    wsel = lambda shape, which: pl.BlockSpec(
        (None,) + shape, lambda i, e1r, e2r, nvr: ((e1r, e2r)[which][i], 0, 0))
    grid_spec = pltpu.PrefetchScalarGridSpec(
        num_scalar_prefetch=3,
        grid=(n_tiles,),
        in_specs=[
            smem_blk(lambda i, *_: (i, 0, 0)),
            smem_blk(lambda i, *_: (jnp.minimum(i + 1, n_tiles - 1), 0, 0)),
            pl.BlockSpec(memory_space=pl.ANY),
            wsel((D_MODEL, D_EXPERT), 0), wsel((D_MODEL, D_EXPERT), 0), wsel((D_EXPERT, D_MODEL), 0),
            wsel((D_MODEL, D_EXPERT), 1), wsel((D_MODEL, D_EXPERT), 1), wsel((D_EXPERT, D_MODEL), 1),
        ],
        out_specs=pl.BlockSpec(memory_space=pl.ANY),
        scratch_shapes=[
            pltpu.VMEM((tm, H2_COLS), F32), pltpu.VMEM((tm, H2_COLS), F32),
            pltpu.VMEM((tm, D_MODEL), F32), pltpu.VMEM((tm, D_MODEL), F32),
            pltpu.SemaphoreType.DMA((2,)), pltpu.SemaphoreType.DMA((2,)),
        ],
    )
    return pl.pallas_call(
        functools.partial(_moe_body, tm, n_tiles),
        grid_spec=grid_spec,
        out_shape=jax.ShapeDtypeStruct((n_tok, D_MODEL), F32),
        compiler_params=_cparams(1),
        name="moe",
    )(e1, e2, nv, tok, tok, h2_all, wgate, wup, wdown, wgate, wup, wdown)


def _final_body(x_ref, y_ref, mod_ref, g_ref, o_ref):
    x = x_ref[...] + mod_ref[5:6, :] * y_ref[...]
    o_ref[...] = _rms(x, g_ref[...])


def _final(layer, x_all, y_all, mod4, final_g, n_lat, seq):
    tt = TT_PROJ
    tiles_per_batch = seq // tt
    row_spec = pl.BlockSpec((tt, D_MODEL), lambda i: (i, 0))
    return pl.pallas_call(
        _final_body,
        grid=(n_lat // tt,),
        in_specs=[row_spec, row_spec,
                  pl.BlockSpec((None, None, 6, D_MODEL), lambda i: (layer, i // tiles_per_batch, 0, 0)),
                  pl.BlockSpec((1, D_MODEL), lambda i: (0, 0))],
        out_specs=row_spec,
        out_shape=jax.ShapeDtypeStruct((n_lat, D_MODEL), F32),
        compiler_params=_cparams(1),
        name="final_norm",
    )(x_all, y_all, mod4, final_g)


def _rope_tables(seq, pad_rows):
    rows = seq // GRID_W
    row = jnp.repeat(jnp.arange(rows, dtype=F32), GRID_W)
    col = jnp.tile(jnp.arange(GRID_W, dtype=F32), rows)
    inv = ROPE_BASE ** (-jnp.arange(QUARTER, dtype=F32) / QUARTER)
    ang = jnp.stack([row[:, None] * inv, col[:, None] * inv], axis=1)
    cos, sin = jnp.cos(ang), jnp.sin(ang)
    zero = jnp.zeros_like(sin)
    c64 = jnp.stack([cos, cos], axis=2).reshape(seq, HEAD_DIM)
    s1 = jnp.stack([-sin, zero], axis=2).reshape(seq, HEAD_DIM)
    s2 = jnp.stack([zero, sin], axis=2).reshape(seq, HEAD_DIM)
    out = []
    for t, fill in ((c64, 1.0), (s1, 0.0), (s2, 0.0)):
        t = jnp.concatenate([t, t], axis=1)
        out.append(jnp.concatenate([t, jnp.full((pad_rows, LANES), fill, F32)], axis=0))
    return out


def _permute_w_in(w_in):
    return jnp.concatenate(
        [w_in[..., 768:1024], w_in[..., 1024:1280], w_in[..., 1792:1920], w_in[..., 1920:2048],
         w_in[..., 0:256], w_in[..., 256:512], w_in[..., 512:768], w_in[..., 1280:1792]], axis=-1)


def kernel(x, c, ctx, c_ctx, w_mod, b_mod, norm1_g, norm2_g, w_in, gmlp_g, w_s, b_s, conv_w,
           attn_sink, w_out, router_w, router_b, w_gate, w_up, w_down, final_g):
    batch, seq, d = x.shape
    ctx_len = ctx.shape[1]
    depth = w_mod.shape[0]
    n_lat = batch * seq
    n_ctx = batch * ctx_len
    assert d == D_MODEL and seq % TT_MIX == 0 and n_ctx % TT_PROJ == 0 and n_ctx % TT_MIX == 0
    assert TT_MIX % ctx_len == 0 and ctx_len % BLOCK == 0 and n_lat % ctx_len == 0
    assert batch < MOD_ROWS

    c_all = jnp.concatenate(
        [c, c_ctx[None], jnp.zeros((MOD_ROWS - batch - 1, d), F32)], axis=0)
    mod4 = _modulation(c_all, w_mod, b_mod).reshape(depth, MOD_ROWS, 6, d)

    rope_tabs = _rope_tables(seq, TT_PROJ)
    win_b = _permute_w_in(w_in).astype(BF16)
    wout_b = w_out.astype(BF16)
    ws_b = w_s.astype(BF16)
    wg_b, wu_b, wd_b = w_gate.astype(BF16), w_up.astype(BF16), w_down.astype(BF16)
    rwt = router_w.T.astype(BF16)
    rb = router_b.reshape(N_EXPERTS, 1).astype(F32)
    bs_full = jnp.repeat(jnp.swapaxes(b_s, 1, 2), W_A // A_GROUPS, axis=2)
    cw_pad = jnp.concatenate([conv_w, jnp.zeros((depth, 5, W_B), F32)], axis=1)

    x_all = (x.reshape(n_lat, d), ctx.reshape(n_ctx, d))
    y_all = None
    for l in range(depth):
        last = l == depth - 1
        x_all, p_all = _inproj(l, x_all, y_all, mod4, norm1_g[l][None], gmlp_g[l][None], win_b[l],
                               rope_tabs, n_lat, n_lat + n_ctx, seq, batch)
        wts = (ws_b[l], bs_full[l], cw_pad[l], attn_sink[l], wout_b[l], norm2_g[l][None], rwt, rb)
        x_all, h2, ro = _mixer(l, not last, p_all, x_all, mod4, wts, n_lat, seq, batch, ctx_len)
        n_tok = h2.shape[0]
        cls = ro[:, 0, :].reshape(-1).astype(jnp.int32)
        plan = _route_plan(cls, n_tok, TM_MOE)
        y_all = _moe(h2, plan, wg_b[l], wu_b[l], wd_b[l], n_tok)
    out = _final(depth - 1, x_all, y_all, mod4, final_g[None], n_lat, seq)
    return out.reshape(batch, seq, d)
```
